```python
import math
import jax
import jax.numpy as jnp
from jax import lax
import numpy as np

D_MODEL = 1024
BATCH = 2
SEQ = 8192
DEPTH = 4
DEC_BATCH = 32
DEC_SEQ = 8
PAST_LEN = 8192
PAGE_SIZE = 128

A_WINDOWS = (128, 512, 2048)
A_DILATIONS = (1, 4, 16)
A_GROUPS = 3
A_HEADS = 8
A_HEAD_DIM = 64
A_GROUP_WIDTH = A_HEADS * A_HEAD_DIM
A_QKV_WIDTH = A_GROUPS * A_GROUP_WIDTH
A_OUT_WIDTH = A_GROUP_WIDTH
NUM_BUCKETS = 32
MAX_DISTANCE = 2048
B_HEADS = 4
B_KDIM = 128
B_VDIM = 128
B_WIDTH = B_HEADS * B_KDIM
HGRN_CHUNK = 64
FF_DIM = ((8 * D_MODEL // 3 + 255) // 256) * 256
N_IN = 3 * A_QKV_WIDTH + 4 * B_WIDTH + 2 * D_MODEL
ALPHA = (2 * DEPTH) ** 0.25
BETA = (8 * DEPTH) ** -0.25
LN_EPS = 1e-5
RMS_EPS = 1e-6
NEG = -1e30
TINY = 1e-30

kernel_name = 'hybrid_dilated_attn_hgrn2_deepnorm_decoder_step'


def layer_norm(x, g=None, b=None):
    xf = x.astype(jnp.float32)
    mu = jnp.mean(xf, axis=-1, keepdims=True)
    var = jnp.mean(jnp.square(xf - mu), axis=-1, keepdims=True)
    y = (xf - mu) * lax.rsqrt(var + LN_EPS)
    if g is not None:
        y = y * g.astype(jnp.float32) + b.astype(jnp.float32)
    return y.astype(x.dtype)


def split_heads(t, nh):
    return t.reshape(t.shape[0], t.shape[1], nh, -1)


def t5_bucket(n):
    max_exact = NUM_BUCKETS // 2
    nf = jnp.maximum(n.astype(jnp.float32), float(max_exact))
    large = max_exact + (jnp.log(nf / max_exact) / math.log(MAX_DISTANCE / max_exact)
                         * (NUM_BUCKETS - max_exact)).astype(jnp.int32)
    large = jnp.minimum(large, NUM_BUCKETS - 1)
    return jnp.where(n < max_exact, n, large)


def group_bias(rel_bias, g):
    dil = A_DILATIONS[g]
    nk = A_WINDOWS[g] // dil + 1
    dist = jnp.arange(nk, dtype=jnp.int32) * dil
    tab = rel_bias[t5_bucket(dist)][:, g * A_HEADS:(g + 1) * A_HEADS]
    return tab.T.astype(jnp.float32)


def dilated_attn_prompt(q, k, v, window, dil, bias):
    b, s, h, hd = q.shape
    band = window // dil
    unit = dil * band
    s_pad = -(-s // unit) * unit
    m = s_pad // dil
    nb = m // band

    def split(t):
        t = jnp.pad(t, ((0, 0), (0, s_pad - s), (0, 0), (0, 0)))
        t = t.reshape(b, m, dil, h, hd).transpose(0, 2, 1, 3, 4)
        return t.reshape(b, dil, nb, band, h, hd)

    def with_prev(t):
        prev = jnp.pad(t, ((0, 0), (0, 0), (1, 0), (0, 0), (0, 0), (0, 0)))[:, :, :-1]
        return jnp.concatenate([prev, t], axis=3)

    qb = split(q)
    kk = with_prev(split(k))
    vv = with_prev(split(v))
    scores = jnp.einsum('brnqhd,brnkhd->brnhqk', qb, kk).astype(jnp.float32) * (hd ** -0.5)
    qi = jnp.arange(band)[:, None]
    kj = jnp.arange(2 * band)[None, :]
    rel = qi + band - kj
    key_sub = jnp.arange(nb)[:, None, None] * band - band + kj[None]
    valid = (rel >= 0) & (rel <= band) & (key_sub >= 0)
    logits = scores + bias[:, jnp.clip(rel, 0, band)]
    logits = jnp.where(valid[:, None], logits, NEG)
    lse = jax.nn.logsumexp(logits, axis=-1)
    p = jnp.exp(logits - lse[..., None]).astype(v.dtype)
    o = jnp.einsum('brnhqk,brnkhd->brnqhd', p, vv)
    o = o.reshape(b, dil, m, h, hd).transpose(0, 2, 1, 3, 4).reshape(b, s_pad, h, hd)[:, :s]
    lse = lse.transpose(0, 1, 2, 4, 3).reshape(b, dil, m, h).transpose(0, 2, 1, 3).reshape(b, s_pad, h)[:, :s]
    return o, lse


def dilated_attn_sample(q, k_new, v_new, buf, window, dil, bias):
    b, t, h, hd = q.shape
    lw = buf.shape[2]
    k_all = jnp.concatenate([buf[:, 0], k_new], axis=1)
    v_all = jnp.concatenate([buf[:, 1], v_new], axis=1)
    nk = window // dil + 1
    idx = lw + jnp.arange(t)[:, None] - dil * jnp.arange(nk)[None, :]
    valid = idx >= 0
    idx = jnp.maximum(idx, 0)
    kg = k_all[:, idx]
    vg = v_all[:, idx]
    scores = jnp.einsum('bthd,btkhd->bhtk', q, kg).astype(jnp.float32) * (hd ** -0.5) + bias[:, None, :]
    logits = jnp.where(valid, scores, NEG)
    lse = jax.nn.logsumexp(logits, axis=-1)
    p = jnp.exp(logits - lse[..., None]).astype(v_new.dtype)
    o = jnp.einsum('bhtk,btkhd->bthd', p, vg)
    keep = min(window, lw + t)
    new_buf = jnp.stack([k_all[:, -keep:], v_all[:, -keep:]], axis=1)
    return o, lse.transpose(0, 2, 1), new_buf


def hgrn2_recurrence(q, k, v, log_f, s0):
    b, l, h, _ = q.shape
    dv = v.shape[-1]
    c = math.gcd(l, HGRN_CHUNK)
    nc = l // c

    def chunks(t):
        return t.reshape(b, nc, c, h, t.shape[-1]).transpose(1, 0, 3, 2, 4)

    causal = jnp.tril(jnp.ones((c, c), dtype=bool))

    def step(s, inp):
        qc, kc, vc, gc = inp
        cum = jnp.cumsum(gc, axis=2)
        diff = cum[:, :, :, None, :] - cum[:, :, None, :, :]
        decay = jnp.exp(jnp.where(causal[:, :, None], diff, NEG))
        att = jnp.einsum('bhtd,bhsd,bhtsd->bhts', qc, kc, decay)
        o = jnp.einsum('bhts,bhsv->bhtv', att, vc) + jnp.einsum('bhtd,bhdv->bhtv', qc * jnp.exp(cum), s)
        last = cum[:, :, -1:, :]
        s = jnp.exp(last[:, :, 0, :, None]) * s + jnp.einsum('bhsd,bhsv->bhdv', kc * jnp.exp(last - cum), vc)
        return s, o

    s, o = lax.scan(step, s0, (chunks(q), chunks(k), chunks(v), chunks(log_f)))
    o = o.transpose(1, 0, 3, 2, 4).reshape(b, l, h, dv)
    return o, s


def token_mixer(h, w_in, w_branch_a, w_branch_b, w_out, norm_w, lb, rel_bias, kv_bufs, s0):
    b, l, _ = h.shape
    f32 = jnp.float32
    sizes = [A_QKV_WIDTH] * 3 + [B_WIDTH] * 4 + [D_MODEL] * 2
    cuts = np.cumsum(sizes)[:-1].tolist()
    qa, ka, va, qb, fb, ib, gb, gate_a, gate_b = jnp.split(h @ w_in, cuts, axis=-1)

    outs, lses, new_bufs = [], [], []
    for g in range(A_GROUPS):
        win, dil = A_WINDOWS[g], A_DILATIONS[g]
        bias = group_bias(rel_bias, g)
        sl = slice(g * A_GROUP_WIDTH, (g + 1) * A_GROUP_WIDTH)
        qg = split_heads(qa[..., sl], A_HEADS)
        kg = split_heads(ka[..., sl], A_HEADS)
        vg = split_heads(va[..., sl], A_HEADS)
        if kv_bufs is None:
            o, lse = dilated_attn_prompt(qg, kg, vg, win, dil, bias)
            keep = min(win, l)
            buf = jnp.stack([kg[:, l - keep:], vg[:, l - keep:]], axis=1)
        else:
            o, lse, buf = dilated_attn_sample(qg, kg, vg, kv_bufs[g], win, dil, bias)
        outs.append(o)
        lses.append(lse)
        new_bufs.append(buf)
    wts = jax.nn.softmax(jnp.stack(lses), axis=0)
    y_a = jnp.einsum('gblh,gblhd->blhd', wts, jnp.stack(outs).astype(f32))
    y_a = y_a.reshape(b, l, A_OUT_WIDTH).astype(h.dtype)

    lb = lb.astype(f32)
    ff = fb.astype(f32)
    sig = jax.nn.sigmoid(ff)
    f_gate = lb + (1.0 - lb) * sig
    log_f = jnp.log(jnp.maximum(f_gate, TINY))
    k_in = (1.0 - lb) * (1.0 - sig)
    q_in = jax.nn.silu(qb.astype(f32))
    o_b, s_new = hgrn2_recurrence(split_heads(q_in, B_HEADS), split_heads(k_in, B_HEADS),
                                  split_heads(ib.astype(f32), B_HEADS), split_heads(log_f, B_HEADS),
                                  s0.astype(f32))
    o_b = o_b * lax.rsqrt(jnp.mean(jnp.square(o_b), axis=-1, keepdims=True) + RMS_EPS)
    o_b = o_b * norm_w.astype(f32).reshape(B_HEADS, B_VDIM)
    y_b = (o_b.reshape(b, l, B_WIDTH) * jax.nn.silu(gb.astype(f32))).astype(h.dtype)

    merged = jax.nn.sigmoid(gate_a) * (y_a @ w_branch_a) + jax.nn.sigmoid(gate_b) * (y_b @ w_branch_b)
    return merged @ w_out, new_bufs, s_new.astype(h.dtype)


def swiglu(h, w_up, w_down):
    a, g = jnp.split(h @ w_up, 2, axis=-1)
    return (jax.nn.silu(a) * g) @ w_down


def trunk(x, c, caches, w_in, w_branch_a, w_branch_b, w_out, hgrn_norm_w, lbs, rel_bias,
          ffn_w_up, ffn_w_down, w_ada, b_ada, ln1_g, ln1_b, ln2_g, ln2_b):
    b = x.shape[0]
    new_kv = [[] for _ in range(A_GROUPS)]
    new_s = []
    for layer in range(DEPTH):
        mod = (c @ w_ada[layer] + b_ada[layer])[:, None, :]
        sh1, sc1, gt1, sh2, sc2, gt2 = jnp.split(mod, 6, axis=-1)
        if caches is None:
            bufs = None
            s0 = jnp.zeros((b, B_HEADS, B_KDIM, B_VDIM), jnp.float32)
        else:
            bufs = [cache[layer] for cache in caches[:A_GROUPS]]
            s0 = caches[A_GROUPS][layer]
        h = layer_norm(x) * (1.0 + sc1) + sh1
        mix, bufs_new, s_new = token_mixer(h, w_in[layer], w_branch_a[layer], w_branch_b[layer], w_out[layer],
                                           hgrn_norm_w[layer], lbs[layer], rel_bias, bufs, s0)
        x = layer_norm(ALPHA * x + (1.0 + gt1) * mix, ln1_g[layer], ln1_b[layer])
        h = layer_norm(x) * (1.0 + sc2) + sh2
        x = layer_norm(ALPHA * x + (1.0 + gt2) * swiglu(h, ffn_w_up[layer], ffn_w_down[layer]),
                       ln2_g[layer], ln2_b[layer])
        for g in range(A_GROUPS):
            new_kv[g].append(bufs_new[g])
        new_s.append(s_new)
    return x, jnp.stack(new_kv[0]), jnp.stack(new_kv[1]), jnp.stack(new_kv[2]), jnp.stack(new_s)


def setup_inputs(seed: int = 0) -> dict:
    key = jax.random.key(seed)
    ks = jax.random.split(key, 24)
    f32 = jnp.float32

    def nrm(k, shape, scale):
        return jax.random.normal(k, shape, f32) * scale

    lw = [min(w, PAST_LEN) for w in A_WINDOWS]
    return {
        'x_prompt': nrm(ks[0], (BATCH, SEQ, D_MODEL), 1.0),
        'x_sample': nrm(ks[1], (DEC_BATCH, DEC_SEQ, D_MODEL), 1.0),
        'c_prompt': nrm(ks[2], (BATCH, D_MODEL), 1.0),
        'c_sample': nrm(ks[3], (DEC_BATCH, D_MODEL), 1.0),
        'cache_kv_w128': nrm(ks[4], (DEPTH, DEC_BATCH, 2, lw[0], A_HEADS, A_HEAD_DIM), 1.0),
        'cache_kv_w512': nrm(ks[5], (DEPTH, DEC_BATCH, 2, lw[1], A_HEADS, A_HEAD_DIM), 1.0),
        'cache_kv_w2048': nrm(ks[6], (DEPTH, DEC_BATCH, 2, lw[2], A_HEADS, A_HEAD_DIM), 1.0),
        'state_hgrn': nrm(ks[7], (DEPTH, DEC_BATCH, B_HEADS, B_KDIM, B_VDIM), 0.5),
        'w_in': nrm(ks[8], (DEPTH, D_MODEL, N_IN), D_MODEL ** -0.5),
        'w_branch_a': nrm(ks[9], (DEPTH, A_OUT_WIDTH, D_MODEL), A_OUT_WIDTH ** -0.5),
        'w_branch_b': nrm(ks[10], (DEPTH, B_WIDTH, D_MODEL), B_WIDTH ** -0.5),
        'w_out': nrm(ks[11], (DEPTH, D_MODEL, D_MODEL), BETA * D_MODEL ** -0.5),
        'hgrn_norm_w': 1.0 + nrm(ks[12], (DEPTH, B_WIDTH), 0.02),
        'hgrn_lb_logits': nrm(ks[13], (DEPTH, B_WIDTH), 0.1),
        'rel_bias': nrm(ks[14], (NUM_BUCKETS, A_GROUPS * A_HEADS), 0.5),
        'ffn_w_up': nrm(ks[15], (DEPTH, D_MODEL, 2 * FF_DIM), D_MODEL ** -0.5),
        'ffn_w_down': nrm(ks[16], (DEPTH, FF_DIM, D_MODEL), BETA * FF_DIM ** -0.5),
        'w_ada': nrm(ks[17], (DEPTH, D_MODEL, 6 * D_MODEL), 0.1 * D_MODEL ** -0.5),
        'b_ada': nrm(ks[18], (DEPTH, 6 * D_MODEL), 0.01),
        'ln1_g': 1.0 + nrm(ks[19], (DEPTH, D_MODEL), 0.02),
        'ln1_b': nrm(ks[20], (DEPTH, D_MODEL), 0.02),
        'ln2_g': 1.0 + nrm(ks[21], (DEPTH, D_MODEL), 0.02),
        'ln2_b': nrm(ks[22], (DEPTH, D_MODEL), 0.02),
    }


def reference(x_prompt, x_sample, c_prompt, c_sample, cache_kv_w128, cache_kv_w512, cache_kv_w2048, state_hgrn,
              w_in, w_branch_a, w_branch_b, w_out, hgrn_norm_w, hgrn_lb_logits, rel_bias,
              ffn_w_up, ffn_w_down, w_ada, b_ada, ln1_g, ln1_b, ln2_g, ln2_b):
    p = jax.nn.softmax(hgrn_lb_logits.astype(jnp.float32), axis=0)
    lbs = jnp.cumsum(p, axis=0) - p[0]
    weights = (w_in, w_branch_a, w_branch_b, w_out, hgrn_norm_w, lbs, rel_bias,
               ffn_w_up, ffn_w_down, w_ada, b_ada, ln1_g, ln1_b, ln2_g, ln2_b)
    y_prompt, kv128_p, kv512_p, kv2048_p, s_p = trunk(x_prompt, c_prompt, None, *weights)
    y_sample, kv128_s, kv512_s, kv2048_s, s_s = trunk(
        x_sample, c_sample, (cache_kv_w128, cache_kv_w512, cache_kv_w2048, state_hgrn), *weights)
    return (y_prompt, y_sample, kv128_p, kv512_p, kv2048_p, s_p, kv128_s, kv512_s, kv2048_s, s_s)
```

```python
import functools
import math

import jax
import jax.numpy as jnp
import numpy as np
from jax import lax
from jax.experimental import pallas as pl
from jax.experimental.pallas import tpu as pltpu

F32 = jnp.float32
BF16 = jnp.bfloat16

D_MODEL = 1024
A_WINDOWS = (128, 512, 2048)
A_DILATIONS = (1, 4, 16)
A_GROUPS = 3
A_HEADS = 8
A_HEAD_DIM = 64
A_GROUP_WIDTH = A_HEADS * A_HEAD_DIM
A_QKV_WIDTH = A_GROUPS * A_GROUP_WIDTH
BAND = 128
NUM_BUCKETS = 32
MAX_DISTANCE = 2048
B_HEADS = 4
B_DIM = 128
B_WIDTH = B_HEADS * B_DIM
FF_DIM = 2816
N_IN = 3 * A_QKV_WIDTH + 4 * B_WIDTH + 2 * D_MODEL
N_MOD = 6 * D_MODEL
DEPTH = 4
ALPHA = (2 * DEPTH) ** 0.25
LN_EPS = 1e-5
RMS_EPS = 1e-6
NEG = -1e30
TINY = 1e-30

COL_TILE = 512
QKV_TILES = 3 * A_QKV_WIDTH // COL_TILE
HG_TILES = 4 * B_WIDTH // COL_TILE
GATE_TILES = 2 * D_MODEL // COL_TILE
FF_TILE = 256
HGRN_TILE = 128
DEC_ROWS = 8
SAMPLE_KEYS = BAND + DEC_ROWS
LANE_SLABS = A_GROUP_WIDTH // 128
VMEM_LIMIT = 56 * 1024 * 1024


def _cparams(sem):
    return pltpu.CompilerParams(dimension_semantics=sem, vmem_limit_bytes=VMEM_LIMIT)


def _sigmoid(x):
    return 1.0 / (1.0 + jnp.exp(-x))


def _silu(x):
    return x * _sigmoid(x)


def _ln(x):
    mu = jnp.mean(x, axis=-1, keepdims=True)
    xc = x - mu
    var = jnp.mean(xc * xc, axis=-1, keepdims=True)
    return xc * lax.rsqrt(var + LN_EPS)


def _dot(a, b):
    return jnp.dot(a, b, preferred_element_type=F32)


def _dot_nt(a, b):
    return lax.dot_general(a, b, (((1,), (1,)), ((), ())), preferred_element_type=F32)


def _lbs_kernel(lg_ref, out_ref):
    x = lg_ref[...]
    m = jnp.max(x, axis=0, keepdims=True)
    e = jnp.exp(x - m)
    p = e / jnp.sum(e, axis=0, keepdims=True)
    rows = []
    c = None
    for l in range(x.shape[0]):
        c = p[l:l + 1] if c is None else c + p[l:l + 1]
        rows.append(c - p[0:1])
    out_ref[...] = jnp.concatenate(rows, axis=0)


def _lbs_call(logits):
    return pl.pallas_call(
        _lbs_kernel, out_shape=jax.ShapeDtypeStruct(logits.shape, F32), name="hgrn_lbs",
    )(logits.astype(F32))


def _ada_kernel(c_ref, w_ref, b_ref, o_ref):
    o_ref[0] = _dot(c_ref[...].astype(BF16), w_ref[0].astype(BF16)) + b_ref[0]


def _ada_call(c_all, w_ada, b_ada):
    depth = w_ada.shape[0]
    rows = c_all.shape[0]
    tn = 512
    return pl.pallas_call(
        _ada_kernel,
        grid=(depth, N_MOD // tn),
        in_specs=[
            pl.BlockSpec((rows, D_MODEL), lambda l, j: (0, 0)),
            pl.BlockSpec((1, D_MODEL, tn), lambda l, j: (l, 0, j)),
            pl.BlockSpec((1, 1, tn), lambda l, j: (l, 0, j)),
        ],
        out_specs=pl.BlockSpec((1, rows, tn), lambda l, j: (l, 0, j)),
        out_shape=jax.ShapeDtypeStruct((depth, rows, N_MOD), F32),
        compiler_params=_cparams(("parallel", "parallel")),
        name="ada_mod",
    )(c_all, w_ada, b_ada.reshape(depth, 1, N_MOD))


def _t5_bucket_np(n):
    n = np.asarray(n, np.int32)
    max_exact = NUM_BUCKETS // 2
    nf = np.maximum(n.astype(np.float32), np.float32(max_exact))
    val = (np.log(nf / np.float32(max_exact)) / np.float32(math.log(MAX_DISTANCE / max_exact))
           * np.float32(NUM_BUCKETS - max_exact))
    large = max_exact + val.astype(np.int32)
    large = np.minimum(large, NUM_BUCKETS - 1)
    return np.where(n < max_exact, n, large).astype(np.int32)


def _bias_p_kernel(rb_ref, bk_ref, o_ref):
    g = pl.program_id(0)
    h = pl.program_id(1)
    bk = bk_ref[0]
    acc = jnp.full(bk.shape, NEG, F32)
    for b in range(NUM_BUCKETS):
        acc = jnp.where(bk == b, rb_ref[b, g * A_HEADS + h], acc)
    o_ref[0, 0] = acc


def _bias_p_call(rel_bias):
    qi = np.arange(BAND)[:, None]
    kj = np.arange(2 * BAND)[None, :]
    rel = qi + BAND - kj
    valid = (rel >= 0) & (rel <= BAND)
    bks = []
    for g in range(A_GROUPS):
        bk = _t5_bucket_np(np.clip(rel, 0, BAND) * A_DILATIONS[g])
        bks.append(np.where(valid, bk, -1))
    bk_all = jnp.asarray(np.stack(bks).astype(np.int32))
    return pl.pallas_call(
        _bias_p_kernel,
        grid=(A_GROUPS, A_HEADS),
        in_specs=[
            pl.BlockSpec(memory_space=pltpu.SMEM),
            pl.BlockSpec((1, BAND, 2 * BAND), lambda g, h: (g, 0, 0)),
        ],
        out_specs=pl.BlockSpec((1, 1, BAND, 2 * BAND), lambda g, h: (g, h, 0, 0)),
        out_shape=jax.ShapeDtypeStruct((A_GROUPS, A_HEADS, BAND, 2 * BAND), F32),
        name="bias_prompt",
    )(rel_bias.astype(F32), bk_all)


def _bias_s_kernel(rb_ref, bk_ref, o_ref):
    bk = bk_ref[0]
    acc = jnp.where(bk < 0, NEG, 0.0).astype(F32)
    for b in range(NUM_BUCKETS):
        acc = jnp.where(bk == b, rb_ref[0, b:b + 1, :], acc)
    o_ref[0] = acc


def _bias_s_call(rel_bias):
    offs = np.concatenate([BAND - np.arange(BAND), np.zeros(1, np.int64)])
    bks = []
    for g in range(A_GROUPS):
        bk = _t5_bucket_np(offs * A_DILATIONS[g])
        bk = np.concatenate([bk, -np.ones(SAMPLE_KEYS - BAND - 1, np.int32)])
        bks.append(np.broadcast_to(bk[:, None], (SAMPLE_KEYS, 128)))
    bk_all = jnp.asarray(np.stack(bks).astype(np.int32))
    rb = rel_bias.astype(F32).reshape(NUM_BUCKETS, A_GROUPS, A_HEADS).transpose(1, 0, 2)
    rb = jnp.pad(rb, ((0, 0), (0, 0), (0, 128 - A_HEADS)))
    return pl.pallas_call(
        _bias_s_kernel,
        grid=(A_GROUPS,),
        in_specs=[
            pl.BlockSpec((1, NUM_BUCKETS, 128), lambda g: (g, 0, 0)),
            pl.BlockSpec((1, SAMPLE_KEYS, 128), lambda g: (g, 0, 0)),
        ],
        out_specs=pl.BlockSpec((1, SAMPLE_KEYS, 128), lambda g: (g, 0, 0)),
        out_shape=jax.ShapeDtypeStruct((A_GROUPS, SAMPLE_KEYS, 128), F32),
        name="bias_sample",
    )(rb, bk_all)


def _ln_mm_kernel(x_ref, mod_ref, w_ref, *refs, ranges, j0):
    out_refs = refs[:len(ranges)]
    h_ref = refs[len(ranges)]
    j = pl.program_id(2)

    @pl.when(j == 0)
    def _():
        hn = _ln(x_ref[0])
        sh = mod_ref[0, :, 0:D_MODEL]
        sc = mod_ref[0, :, D_MODEL:2 * D_MODEL]
        h_ref[...] = (hn * (1.0 + sc) + sh).astype(BF16)

    acc = _dot(h_ref[...], w_ref[...])
    for (start, cnt, _), o_ref in zip(ranges, out_refs):
        @pl.when((j + j0 >= start) & (j + j0 < start + cnt))
        def _(o_ref=o_ref):
            o_ref[0] = acc.astype(o_ref.dtype)


def _ln_mm_call(x, mod, w, *, ranges, tm, row_tile_off=0, n_row_tiles=None, name):
    bsz, s, d = x.shape
    tn = COL_TILE
    j0 = min(r[0] for r in ranges)
    nj = max(r[0] + r[1] for r in ranges) - j0
    ni = n_row_tiles if n_row_tiles is not None else s // tm
    per_row = mod.shape[1] != 1
    if per_row:
        mod_spec = pl.BlockSpec((1, tm, N_MOD), lambda b, i, j: (b, i + row_tile_off, 0))
    else:
        mod_spec = pl.BlockSpec((1, 1, N_MOD), lambda b, i, j: (b, 0, 0))
    out_specs = [
        pl.BlockSpec((1, tm, tn), lambda b, i, j, st=st, c=c: (b, i, jnp.clip(j + j0 - st, 0, c - 1)))
        for (st, c, _) in ranges
    ]
    out_shape = [jax.ShapeDtypeStruct((bsz, ni * tm, c * tn), dt) for (_, c, dt) in ranges]
    return pl.pallas_call(
        functools.partial(_ln_mm_kernel, ranges=ranges, j0=j0),
        grid=(bsz, ni, nj),
        in_specs=[
            pl.BlockSpec((1, tm, d), lambda b, i, j: (b, i + row_tile_off, 0)),
            mod_spec,
            pl.BlockSpec((d, tn), lambda b, i, j: (0, j + j0)),
        ],
        out_specs=out_specs,
        out_shape=out_shape,
        scratch_shapes=[pltpu.VMEM((tm, d), BF16)],
        compiler_params=_cparams(("parallel", "parallel", "arbitrary")),
        name=name,
    )(x, mod, w)


def _attn_p_kernel(q_ref, kp_ref, ko_ref, vp_ref, vo_ref, bias_ref, o_ref, l_ref):
    n = pl.program_id(2)
    q = q_ref[0]
    kcat = jnp.concatenate([kp_ref[0], ko_ref[0]], axis=0)
    vcat = jnp.concatenate([vp_ref[0], vo_ref[0]], axis=0)
    col = lax.broadcasted_iota(jnp.int32, (BAND, 2 * BAND), 1)
    no_prev = jnp.where((col < BAND) & (n == 0), NEG, 0.0).astype(F32)
    outs, lses = [], []
    for h in range(A_HEADS):
        sl = slice(h * A_HEAD_DIM, (h + 1) * A_HEAD_DIM)
        s = _dot_nt(q[:, sl], kcat[:, sl]) * (A_HEAD_DIM ** -0.5) + bias_ref[0, h] + no_prev
        m = jnp.max(s, axis=-1, keepdims=True)
        p = jnp.exp(s - m)
        l = jnp.sum(p, axis=-1, keepdims=True)
        o = _dot(p.astype(BF16), vcat[:, sl]) / l
        outs.append(o)
        lses.append(jnp.broadcast_to(m + jnp.log(l), (BAND, A_HEAD_DIM)))
    o_ref[0] = jnp.concatenate(outs, axis=1).astype(o_ref.dtype)
    l_ref[0] = jnp.concatenate(lses, axis=1)


def _attn_p_call(qkv, bias_p, g):
    bsz, s, w = qkv.shape
    dil = A_DILATIONS[g]
    m = s // dil
    nb = m // BAND
    nct = w // A_GROUP_WIDTH
    qv = qkv.reshape(bsz, m, dil * w)
    blk = (1, BAND, A_GROUP_WIDTH)

    def col(which):
        return lambda b, r, n: (b, n, r * nct + which * A_GROUPS + g)

    def col_prev(which):
        return lambda b, r, n: (b, jnp.maximum(n - 1, 0), r * nct + which * A_GROUPS + g)

    o, lse = pl.pallas_call(
        _attn_p_kernel,
        grid=(bsz, dil, nb),
        in_specs=[
            pl.BlockSpec(blk, col(0)),
            pl.BlockSpec(blk, col_prev(1)),
            pl.BlockSpec(blk, col(1)),
            pl.BlockSpec(blk, col_prev(2)),
            pl.BlockSpec(blk, col(2)),
            pl.BlockSpec((1, A_HEADS, BAND, 2 * BAND), lambda b, r, n: (g, 0, 0, 0)),
        ],
        out_specs=[
            pl.BlockSpec(blk, lambda b, r, n: (b, n, r)),
            pl.BlockSpec(blk, lambda b, r, n: (b, n, r)),
        ],
        out_shape=[
            jax.ShapeDtypeStruct((bsz, m, dil * A_GROUP_WIDTH), BF16),
            jax.ShapeDtypeStruct((bsz, m, dil * A_GROUP_WIDTH), F32),
        ],
        compiler_params=_cparams(("parallel", "parallel", "arbitrary")),
        name=f"attn_prompt_g{g}",
    )(qv, qv, qv, qv, qv, bias_p)
    return o.reshape(bsz, s, A_GROUP_WIDTH), lse.reshape(bsz, s, A_GROUP_WIDTH)


def _attn_s_kernel(q_ref, new_ref, cache_ref, bias_ref, *refs, lw, dil, aliased):
    if aliased:
        refs = refs[1:]
    out_ref, o_ref, lse_ref, all_ref, p_ref = refs
    ph = pl.program_id(1)
    new = new_ref[0]
    chunk = 128

    def copy_in(i, c):
        r = pl.multiple_of(i * chunk, chunk)
        for lc in range(LANE_SLABS):
            all_ref[lc, pl.ds(r, chunk), :] = cache_ref[pl.ds(r, chunk), lc * 128:(lc + 1) * 128]
        return c

    lax.fori_loop(0, lw // chunk, copy_in, 0)
    for lc in range(LANE_SLABS):
        all_ref[lc, lw:lw + DEC_ROWS, :] = new[:, lc * 128:(lc + 1) * 128]

    def copy_out(i, c):
        r = pl.multiple_of(i * chunk, chunk)
        for lc in range(LANE_SLABS):
            out_ref[pl.ds(r, chunk), lc * 128:(lc + 1) * 128] = all_ref[lc, pl.ds(r + DEC_ROWS, chunk), :]
        return c

    lax.fori_loop(0, lw // chunk, copy_out, 0)

    def strided_rows(t):
        return jnp.concatenate(
            [all_ref[lc, pl.ds(t, BAND, stride=dil), :] for lc in range(LANE_SLABS)], axis=1)

    row8 = lax.broadcasted_iota(jnp.int32, (DEC_ROWS, A_GROUP_WIDTH), 0)

    @pl.when(ph == 0)
    def _():
        ci = lax.broadcasted_iota(jnp.int32, (A_GROUP_WIDTH, 128), 0)
        hi = lax.broadcasted_iota(jnp.int32, (A_GROUP_WIDTH, 128), 1)
        e = jnp.where((ci >> 6) == hi, 1.0, 0.0).astype(BF16)
        q = q_ref[0]
        lane = lax.broadcasted_iota(jnp.int32, (DEC_ROWS, A_GROUP_WIDTH), 1)
        lse_x = jnp.zeros((DEC_ROWS, A_GROUP_WIDTH), F32)
        for t in range(DEC_ROWS):
            qt = q[t:t + 1, :]
            keys = strided_rows(t)
            own = jnp.where(row8 == 0, new[t:t + 1, :] * qt, 0.0)
            prod = jnp.concatenate([keys * qt, own], axis=0)
            hi_part = prod.astype(BF16)
            lo_part = (prod - hi_part.astype(F32)).astype(BF16)
            s = (_dot(hi_part, e) + _dot(lo_part, e)) * (A_HEAD_DIM ** -0.5) + bias_ref[0]
            m = jnp.max(s, axis=0, keepdims=True)
            p = jnp.exp(s - m)
            l = jnp.sum(p, axis=0, keepdims=True)
            p_ref[t] = p / l
            lse = m + jnp.log(l)
            lse_row = jnp.zeros((1, A_GROUP_WIDTH), F32)
            for h in range(A_HEADS):
                lse_row = jnp.where((lane[0:1] >> 6) == h, lse[:, h:h + 1], lse_row)
            lse_x = jnp.where(row8 == t, lse_row, lse_x)
        lse_ref[0] = lse_x

    @pl.when(ph == 1)
    def _():
        hi = lax.broadcasted_iota(jnp.int32, (128, A_GROUP_WIDTH), 0)
        ci = lax.broadcasted_iota(jnp.int32, (128, A_GROUP_WIDTH), 1)
        et = jnp.where((ci >> 6) == hi, 1.0, 0.0).astype(BF16)
        o_x = jnp.zeros((DEC_ROWS, A_GROUP_WIDTH), F32)
        for t in range(DEC_ROWS):
            vals = strided_rows(t)
            own = jnp.where(row8 == 0, new[t:t + 1, :], 0.0)
            vall = jnp.concatenate([vals, own], axis=0)
            pe = _dot(p_ref[t].astype(BF16), et)
            o_t = jnp.sum(pe * vall, axis=0, keepdims=True)
            o_x = jnp.where(row8 == t, o_t, o_x)
        o_ref[0] = o_x


def _attn_s_call(qkv_s, cache, prev_out, bias_s, layer, g):
    depth, bd, _, lw, _ = cache.shape
    dil = A_DILATIONS[g]
    assert lw == A_WINDOWS[g] and lw % 128 == 0
    aliased = prev_out is not None
    in_specs = [
        pl.BlockSpec((1, DEC_ROWS, A_GROUP_WIDTH), lambda b, ph: (b, 0, g)),
        pl.BlockSpec((1, DEC_ROWS, A_GROUP_WIDTH), lambda b, ph: (b, 0, A_GROUPS * (1 + ph) + g)),
        pl.BlockSpec((None, None, None, lw, A_GROUP_WIDTH), lambda b, ph: (layer, b, ph, 0, 0)),
        pl.BlockSpec((1, SAMPLE_KEYS, 128), lambda b, ph: (g, 0, 0)),
    ]
    args = [qkv_s, qkv_s, cache, bias_s]
    aliases = {}
    if aliased:
        in_specs.append(pl.BlockSpec(memory_space=pl.ANY))
        args.append(prev_out)
        aliases = {4: 0}
    out, o, lse = pl.pallas_call(
        functools.partial(_attn_s_kernel, lw=lw, dil=dil, aliased=aliased),
        grid=(bd, 2),
        in_specs=in_specs,
        out_specs=[
            pl.BlockSpec((None, None, None, lw, A_GROUP_WIDTH), lambda b, ph: (layer, b, ph, 0, 0)),
            pl.BlockSpec((1, DEC_ROWS, A_GROUP_WIDTH), lambda b, ph: (b, 0, 0)),
            pl.BlockSpec((1, DEC_ROWS, A_GROUP_WIDTH), lambda b, ph: (b, 0, 0)),
        ],
        out_shape=[
            jax.ShapeDtypeStruct(cache.shape, F32),
            jax.ShapeDtypeStruct((bd, DEC_ROWS, A_GROUP_WIDTH), F32),
            jax.ShapeDtypeStruct((bd, DEC_ROWS, A_GROUP_WIDTH), F32),
        ],
        scratch_shapes=[
            pltpu.VMEM((LANE_SLABS, lw + DEC_ROWS, 128), F32),
            pltpu.VMEM((DEC_ROWS, SAMPLE_KEYS, 128), F32),
        ],
        input_output_aliases=aliases,
        compiler_params=_cparams(("parallel", "arbitrary")),
        name=f"attn_sample_g{g}",
    )(*args)
    return out, o, lse


def _hgrn_head(q, k, g, v, st, tt):
    ti = lax.broadcasted_iota(jnp.int32, (tt, tt), 0)
    si = lax.broadcasted_iota(jnp.int32, (tt, tt), 1)
    row = lax.broadcasted_iota(jnp.int32, (tt, B_DIM), 0)
    a = jnp.where(ti == si, _dot_nt(q.astype(BF16), k.astype(BF16)), 0.0)
    c = g
    r = jnp.zeros_like(g)
    lg = 0
    while (1 << lg) < tt:
        b = 1 << lg
        ab = _dot_nt((q * jnp.exp(c)).astype(BF16), (k * jnp.exp(r)).astype(BF16))
        pair = ((ti >> lg) == (si >> lg) + 1) & ((ti >> (lg + 1)) == (si >> (lg + 1)))
        a = jnp.where(pair, ab, a)
        tot = c + r
        second = ((row >> lg) & 1) == 1
        c = c + jnp.where(second, pltpu.roll(tot, b, 0), 0.0)
        r = r + jnp.where(second, 0.0, pltpu.roll(tot, tt - b, 0))
        lg += 1
    o = _dot(a.astype(BF16), v.astype(BF16)) + _dot_nt((q * jnp.exp(c)).astype(BF16), st.astype(BF16))
    dec = jnp.exp(c[tt - 1:tt, :])
    st_new = st * dec + _dot(v.T.astype(BF16), (k * jnp.exp(r)).astype(BF16))
    return o, st_new


def _hgrn_kernel(qb_ref, fb_ref, ib_ref, gb_ref, lb_ref, nw_ref, *refs, tt, valid, has_s0):
    if has_s0:
        s0_ref, y_ref, s_out_ref, st_ref = refs
    else:
        y_ref, s_out_ref, st_ref = refs
    t = pl.program_id(1)
    nt = pl.num_programs(1)

    @pl.when(t == 0)
    def _():
        for hh in range(B_HEADS):
            if has_s0:
                st_ref[hh] = s0_ref[hh].T
            else:
                st_ref[hh] = jnp.zeros((B_DIM, B_DIM), F32)

    def load(ref):
        x = ref[0]
        if valid < tt:
            x = jnp.concatenate([x, jnp.zeros((tt - valid, x.shape[1]), F32)], axis=0)
        return x

    qb, fb, ib, gb = load(qb_ref), load(fb_ref), load(ib_ref), load(gb_ref)
    live = lax.broadcasted_iota(jnp.int32, (tt, B_DIM), 0) < valid
    for hh in range(B_HEADS):
        cs = slice(hh * B_DIM, (hh + 1) * B_DIM)
        lb = lb_ref[:, cs]
        sig = _sigmoid(fb[:, cs])
        g = jnp.log(jnp.maximum(lb + (1.0 - lb) * sig, TINY))
        k = (1.0 - lb) * (1.0 - sig)
        if valid < tt:
            g = jnp.where(live, g, 0.0)
            k = jnp.where(live, k, 0.0)
        o, st_new = _hgrn_head(_silu(qb[:, cs]), k, g, ib[:, cs], st_ref[hh], tt)
        st_ref[hh] = st_new
        o = o * lax.rsqrt(jnp.mean(o * o, axis=-1, keepdims=True) + RMS_EPS) * nw_ref[:, cs]
        y = o * _silu(gb[:, cs])
        y_ref[0, :, cs] = y[:valid].astype(y_ref.dtype)

    @pl.when(t == nt - 1)
    def _():
        for hh in range(B_HEADS):
            s_out_ref[0, hh] = st_ref[hh].T


def _hgrn_call(hg, lb, nw, s0, *, layer, seq_rows, name):
    bsz, s, _ = hg.shape
    tt = HGRN_TILE
    valid = min(seq_rows, tt)
    nt = s // valid
    has_s0 = s0 is not None
    blk = (1, valid, B_WIDTH)
    in_specs = [pl.BlockSpec(blk, lambda b, t, c=c: (b, t, c)) for c in range(4)]
    in_specs += [pl.BlockSpec((1, B_WIDTH), lambda b, t: (0, 0))] * 2
    args = [hg, hg, hg, hg, lb, nw]
    if has_s0:
        in_specs.append(pl.BlockSpec((None, None, B_HEADS, B_DIM, B_DIM), lambda b, t: (layer, b, 0, 0, 0)))
        args.append(s0)
    y, s_out = pl.pallas_call(
        functools.partial(_hgrn_kernel, tt=tt, valid=valid, has_s0=has_s0),
        grid=(bsz, nt),
        in_specs=in_specs,
        out_specs=[
            pl.BlockSpec(blk, lambda b, t: (b, t, 0)),
            pl.BlockSpec((1, B_HEADS, B_DIM, B_DIM), lambda b, t: (b, 0, 0, 0)),
        ],
        out_shape=[
            jax.ShapeDtypeStruct((bsz, s, B_WIDTH), BF16),
            jax.ShapeDtypeStruct((bsz, B_HEADS, B_DIM, B_DIM), F32),
        ],
        scratch_shapes=[pltpu.VMEM((B_HEADS, B_DIM, B_DIM), F32)],
        compiler_params=_cparams(("parallel", "arbitrary")),
        name=name,
    )(*args)
    return y, s_out


def _merge_kernel(o0_ref, o1_ref, o2_ref, l0_ref, l1_ref, l2_ref, yb_ref, gate_ref, x_ref, mod_ref,
                  wa_ref, wb_ref, wo_ref, g_ref, b_ref, out_ref):
    l0, l1, l2 = l0_ref[0], l1_ref[0], l2_ref[0]
    m = jnp.maximum(jnp.maximum(l0, l1), l2)
    e0, e1, e2 = jnp.exp(l0 - m), jnp.exp(l1 - m), jnp.exp(l2 - m)
    ya = (e0 * o0_ref[0].astype(F32) + e1 * o1_ref[0].astype(F32) + e2 * o2_ref[0].astype(F32)) / (e0 + e1 + e2)
    br_a = _dot(ya.astype(BF16), wa_ref[...])
    br_b = _dot(yb_ref[0].astype(BF16), wb_ref[...])
    merged = _sigmoid(gate_ref[0, :, 0:D_MODEL]) * br_a + _sigmoid(gate_ref[0, :, D_MODEL:2 * D_MODEL]) * br_b
    mix = _dot(merged.astype(BF16), wo_ref[...])
    gt = mod_ref[0, :, 2 * D_MODEL:3 * D_MODEL]
    z = ALPHA * x_ref[0] + (1.0 + gt) * mix
    out_ref[0] = _ln(z) * g_ref[...] + b_ref[...]


def _merge_call(os, lses, yb, gates, x, mod, wa, wb, wo, ln_g, ln_b, *, tm, name):
    bsz, s, d = x.shape
    per_row = mod.shape[1] != 1
    tok = lambda w: pl.BlockSpec((1, tm, w), lambda b, i: (b, i, 0))
    mod_spec = (pl.BlockSpec((1, tm, N_MOD), lambda b, i: (b, i, 0)) if per_row
                else pl.BlockSpec((1, 1, N_MOD), lambda b, i: (b, 0, 0)))
    full = lambda a: pl.BlockSpec(a.shape, lambda b, i: (0, 0))
    return pl.pallas_call(
        _merge_kernel,
        grid=(bsz, s // tm),
        in_specs=[tok(A_GROUP_WIDTH)] * 6 + [tok(B_WIDTH), tok(2 * D_MODEL), tok(d), mod_spec,
                                             full(wa), full(wb), full(wo), full(ln_g), full(ln_b)],
        out_specs=tok(d),
        out_shape=jax.ShapeDtypeStruct((bsz, s, d), F32),
        compiler_params=_cparams(("parallel", "parallel")),
        name=name,
    )(*os, *lses, yb, gates, x, mod, wa, wb, wo, ln_g, ln_b)


def _ffn_kernel(x_ref, mod_ref, wua_ref, wug_ref, wd_ref, g_ref, b_ref, out_ref, h_ref, acc_ref):
    j = pl.program_id(2)
    nf = pl.num_programs(2)

    @pl.when(j == 0)
    def _():
        sh = mod_ref[0, :, 3 * D_MODEL:4 * D_MODEL]
        sc = mod_ref[0, :, 4 * D_MODEL:5 * D_MODEL]
        h_ref[...] = (_ln(x_ref[0]) * (1.0 + sc) + sh).astype(BF16)
        acc_ref[...] = jnp.zeros_like(acc_ref)

    h = h_ref[...]
    a = _dot(h, wua_ref[...])
    gg = _dot(h, wug_ref[...])
    acc_ref[...] += _dot((_silu(a) * gg).astype(BF16), wd_ref[...])

    @pl.when(j == nf - 1)
    def _():
        gt = mod_ref[0, :, 5 * D_MODEL:6 * D_MODEL]
        z = ALPHA * x_ref[0] + (1.0 + gt) * acc_ref[...]
        out_ref[0] = _ln(z) * g_ref[...] + b_ref[...]


def _ffn_call(x, mod, w_up, w_down, ln_g, ln_b, *, tm, name):
    bsz, s, d = x.shape
    nf = FF_DIM // FF_TILE
    per_row = mod.shape[1] != 1
    mod_spec = (pl.BlockSpec((1, tm, N_MOD), lambda b, i, j: (b, i, 0)) if per_row
                else pl.BlockSpec((1, 1, N_MOD), lambda b, i, j: (b, 0, 0)))
    return pl.pallas_call(
        _ffn_kernel,
        grid=(bsz, s // tm, nf),
        in_specs=[
            pl.BlockSpec((1, tm, d), lambda b, i, j: (b, i, 0)),
            mod_spec,
            pl.BlockSpec((d, FF_TILE), lambda b, i, j: (0, j)),
            pl.BlockSpec((d, FF_TILE), lambda b, i, j: (0, j + nf)),
            pl.BlockSpec((FF_TILE, d), lambda b, i, j: (j, 0)),
            pl.BlockSpec((1, d), lambda b, i, j: (0, 0)),
            pl.BlockSpec((1, d), lambda b, i, j: (0, 0)),
        ],
        out_specs=pl.BlockSpec((1, tm, d), lambda b, i, j: (b, i, 0)),
        out_shape=jax.ShapeDtypeStruct((bsz, s, d), F32),
        scratch_shapes=[pltpu.VMEM((tm, d), BF16), pltpu.VMEM((tm, d), F32)],
        compiler_params=_cparams(("parallel", "parallel", "arbitrary")),
        name=name,
    )(x, mod, w_up, w_up, w_down, ln_g, ln_b)


_IN_RANGES_PROMPT = ((0, QKV_TILES, BF16), (QKV_TILES, HG_TILES, F32), (QKV_TILES + HG_TILES, GATE_TILES, F32))
_IN_RANGES_SAMPLE = ((0, QKV_TILES, F32), (QKV_TILES, HG_TILES, F32), (QKV_TILES + HG_TILES, GATE_TILES, F32))
_KV_RANGE = ((A_GROUPS, 2 * A_GROUPS, F32),)


def kernel(x_prompt, x_sample, c_prompt, c_sample, cache_kv_w128, cache_kv_w512, cache_kv_w2048, state_hgrn,
           w_in, w_branch_a, w_branch_b, w_out, hgrn_norm_w, hgrn_lb_logits, rel_bias,
           ffn_w_up, ffn_w_down, w_ada, b_ada, ln1_g, ln1_b, ln2_g, ln2_b):
    depth = w_in.shape[0]
    bp, sp, d = x_prompt.shape
    bd, sd, _ = x_sample.shape
    assert sd == DEC_ROWS and d == D_MODEL
    max_win = A_WINDOWS[-1]
    assert sp % max_win == 0 and sp >= max_win
    tm_p = 1024
    tm_s = bd * sd

    w_in_b = w_in.astype(BF16)
    wa_b = w_branch_a.astype(BF16)
    wb_b = w_branch_b.astype(BF16)
    wo_b = w_out.astype(BF16)
    wup_b = ffn_w_up.astype(BF16)
    wdn_b = ffn_w_down.astype(BF16)

    lbs = _lbs_call(hgrn_lb_logits)
    n_c = bp + bd
    n_c_pad = -(-n_c // 8) * 8
    c_all = jnp.concatenate([c_prompt, c_sample, jnp.zeros((n_c_pad - n_c, d), F32)], axis=0)
    mod_all = _ada_call(c_all, w_ada, b_ada)
    bias_p = _bias_p_call(rel_bias)
    bias_s = _bias_s_call(rel_bias)

    caches = [c.reshape(c.shape[:4] + (A_GROUP_WIDTH,)) for c in (cache_kv_w128, cache_kv_w512, cache_kv_w2048)]
    new_caches = [None] * A_GROUPS
    kv_p = [[] for _ in range(A_GROUPS)]
    s_p, s_s = [], []

    xp = x_prompt
    xs = x_sample.reshape(1, tm_s, d)
    for l in range(depth):
        mod_p = mod_all[l, 0:bp].reshape(bp, 1, N_MOD)
        mod_s = jnp.repeat(mod_all[l, bp:bp + bd], sd, axis=0).reshape(1, tm_s, N_MOD)
        lb = lbs[l:l + 1]
        nw = hgrn_norm_w[l:l + 1].astype(F32)
        g1, b1 = ln1_g[l:l + 1], ln1_b[l:l + 1]
        g2, b2 = ln2_g[l:l + 1], ln2_b[l:l + 1]

        qkv, hg, gates = _ln_mm_call(xp, mod_p, w_in_b[l], ranges=_IN_RANGES_PROMPT, tm=tm_p, name="in_proj_prompt")
        (kv_tail,) = _ln_mm_call(xp, mod_p, w_in_b[l], ranges=_KV_RANGE, tm=tm_p,
                                 row_tile_off=(sp - max_win) // tm_p, n_row_tiles=max_win // tm_p,
                                 name="kv_tail_prompt")
        os, lses = [], []
        for g in range(A_GROUPS):
            o, lse = _attn_p_call(qkv, bias_p, g)
            os.append(o)
            lses.append(lse)
            keep = min(A_WINDOWS[g], sp)
            kt = kv_tail[:, max_win - keep:, g * A_GROUP_WIDTH:(g + 1) * A_GROUP_WIDTH]
            vt = kv_tail[:, max_win - keep:, A_QKV_WIDTH + g * A_GROUP_WIDTH:A_QKV_WIDTH + (g + 1) * A_GROUP_WIDTH]
            kv_p[g].append(jnp.stack([kt, vt], axis=1).reshape(bp, 2, keep, A_HEADS, A_HEAD_DIM))
        yb, st = _hgrn_call(hg, lb, nw, None, layer=l, seq_rows=HGRN_TILE, name="hgrn_prompt")
        s_p.append(st)
        xp = _merge_call(os, lses, yb, gates, xp, mod_p, wa_b[l], wb_b[l], wo_b[l], g1, b1, tm=512,
                         name="merge_prompt")
        xp = _ffn_call(xp, mod_p, wup_b[l], wdn_b[l], g2, b2, tm=tm_p, name="ffn_prompt")

        qkv_s, hg_s, gates_s = _ln_mm_call(xs, mod_s, w_in_b[l], ranges=_IN_RANGES_SAMPLE, tm=tm_s,
                                           name="in_proj_sample")
        qkv_s3 = qkv_s.reshape(bd, sd, 3 * A_QKV_WIDTH)
        os, lses = [], []
        for g in range(A_GROUPS):
            new_caches[g], o, lse = _attn_s_call(qkv_s3, caches[g], new_caches[g], bias_s, l, g)
            os.append(o.reshape(1, tm_s, A_GROUP_WIDTH))
            lses.append(lse.reshape(1, tm_s, A_GROUP_WIDTH))
        yb_s, st_s = _hgrn_call(hg_s.reshape(bd, sd, 4 * B_WIDTH), lb, nw, state_hgrn, layer=l, seq_rows=sd,
                                name="hgrn_sample")
        s_s.append(st_s)
        xs = _merge_call(os, lses, yb_s.reshape(1, tm_s, B_WIDTH), gates_s, xs, mod_s, wa_b[l], wb_b[l], wo_b[l],
                         g1, b1, tm=tm_s, name="merge_sample")
        xs = _ffn_call(xs, mod_s, wup_b[l], wdn_b[l], g2, b2, tm=tm_s, name="ffn_sample")

    outs_kv_s = [nc.reshape(nc.shape[:4] + (A_HEADS, A_HEAD_DIM)) for nc in new_caches]
    return (xp, xs.reshape(bd, sd, d),
            jnp.stack(kv_p[0]), jnp.stack(kv_p[1]), jnp.stack(kv_p[2]), jnp.stack(s_p),
            outs_kv_s[0], outs_kv_s[1], outs_kv_s[2], jnp.stack(s_s))
```

```python
import functools
import math

import jax
import jax.numpy as jnp
import numpy as np
from jax import lax
from jax.experimental import pallas as pl
from jax.experimental.pallas import tpu as pltpu

F32 = jnp.float32
BF16 = jnp.bfloat16

D_MODEL = 1024
A_WINDOWS = (128, 512, 2048)
A_DILATIONS = (1, 4, 16)
A_GROUPS = 3
A_HEADS = 8
A_HEAD_DIM = 64
A_GROUP_WIDTH = A_HEADS * A_HEAD_DIM
A_QKV_WIDTH = A_GROUPS * A_GROUP_WIDTH
BAND = 128
NUM_BUCKETS = 32
MAX_DISTANCE = 2048
B_HEADS = 4
B_DIM = 128
B_WIDTH = B_HEADS * B_DIM
FF_DIM = 2816
N_MOD = 6 * D_MODEL
DEPTH = 4
ALPHA = (2 * DEPTH) ** 0.25
LN_EPS = 1e-5
RMS_EPS = 1e-6
NEG = -1e30
TINY = 1e-30
QK_SCALE = A_HEAD_DIM ** -0.5

LANES = 128
HEAD_PAIRS = A_GROUP_WIDTH // LANES
LANE_SLABS = A_GROUP_WIDTH // LANES
FF_CHUNK = FF_DIM // 2
FF_SUBTILES = ((0, 512), (512, 512), (1024, 384))
HGRN_TILE = 128
ATTN_QB = 512
DEC_ROWS = 8
PAD_ROWS = 16
VMEM_LIMIT = 56 * 1024 * 1024


def _cparams(sem):
    return pltpu.CompilerParams(dimension_semantics=sem, vmem_limit_bytes=VMEM_LIMIT)


def _sigmoid(x):
    return 1.0 / (1.0 + jnp.exp(-x))


def _silu(x):
    return x * _sigmoid(x)


def _ln(x):
    mu = jnp.mean(x, axis=-1, keepdims=True)
    xc = x - mu
    var = jnp.mean(xc * xc, axis=-1, keepdims=True)
    return xc * lax.rsqrt(var + LN_EPS)


def _dot(a, b):
    return jnp.dot(a, b, preferred_element_type=F32)


def _dot_nt(a, b):
    return lax.dot_general(a, b, (((1,), (1,)), ((), ())), preferred_element_type=F32)


def _modulated(x_ref, mod_ref, shift_chunk):
    sh = mod_ref[0, :, shift_chunk * D_MODEL:(shift_chunk + 1) * D_MODEL]
    sc = mod_ref[0, :, (shift_chunk + 1) * D_MODEL:(shift_chunk + 2) * D_MODEL]
    return (_ln(x_ref[0]) * (1.0 + sc) + sh).astype(BF16)


def _mod_spec(mod, tm, rank3):
    per_row = mod.shape[1] != 1
    if rank3:
        if per_row:
            return pl.BlockSpec((1, tm, N_MOD), lambda b, i, j: (b, i, 0))
        return pl.BlockSpec((1, 1, N_MOD), lambda b, i, j: (b, 0, 0))
    if per_row:
        return pl.BlockSpec((1, tm, N_MOD), lambda b, i: (b, i, 0))
    return pl.BlockSpec((1, 1, N_MOD), lambda b, i: (b, 0, 0))


def _lbs_kernel(lg_ref, out_ref):
    x = lg_ref[...]
    m = jnp.max(x, axis=0, keepdims=True)
    e = jnp.exp(x - m)
    p = e / jnp.sum(e, axis=0, keepdims=True)
    rows = []
    c = None
    for l in range(x.shape[0]):
        c = p[l:l + 1] if c is None else c + p[l:l + 1]
        rows.append(c - p[0:1])
    out_ref[...] = jnp.concatenate(rows, axis=0)


def _lbs_call(logits):
    return pl.pallas_call(
        _lbs_kernel, out_shape=jax.ShapeDtypeStruct(logits.shape, F32), name="hgrn_lbs",
    )(logits.astype(F32))


def _ada_kernel(c_ref, w_ref, b_ref, o_ref):
    o_ref[0] = _dot(c_ref[...].astype(BF16), w_ref[0].astype(BF16)) + b_ref[0]


def _ada_call(c_all, w_ada, b_ada):
    depth = w_ada.shape[0]
    rows = c_all.shape[0]
    tn = 512
    return pl.pallas_call(
        _ada_kernel,
        grid=(depth, N_MOD // tn),
        in_specs=[
            pl.BlockSpec((rows, D_MODEL), lambda l, j: (0, 0)),
            pl.BlockSpec((1, D_MODEL, tn), lambda l, j: (l, 0, j)),
            pl.BlockSpec((1, 1, tn), lambda l, j: (l, 0, j)),
        ],
        out_specs=pl.BlockSpec((1, rows, tn), lambda l, j: (l, 0, j)),
        out_shape=jax.ShapeDtypeStruct((depth, rows, N_MOD), F32),
        compiler_params=_cparams(("parallel", "parallel")),
        name="ada_mod",
    )(c_all, w_ada, b_ada.reshape(depth, 1, N_MOD))


def _t5_bucket_np(n):
    n = np.asarray(n, np.int32)
    max_exact = NUM_BUCKETS // 2
    nf = np.maximum(n.astype(np.float32), np.float32(max_exact))
    val = (np.log(nf / np.float32(max_exact)) / np.float32(math.log(MAX_DISTANCE / max_exact))
           * np.float32(NUM_BUCKETS - max_exact))
    large = max_exact + val.astype(np.int32)
    large = np.minimum(large, NUM_BUCKETS - 1)
    return np.where(n < max_exact, n, large).astype(np.int32)


def _bias_p_kernel(rb_ref, bk_ref, o_ref):
    g = pl.program_id(0)
    h = pl.program_id(2)
    bk = bk_ref[0, 0]
    acc = jnp.full(bk.shape, NEG, F32)
    for b in range(NUM_BUCKETS):
        acc = jnp.where(bk == b, rb_ref[b, g * A_HEADS + h], acc)
    o_ref[0, 0, 0] = acc


def _bias_p_call(rel_bias):
    qi = np.arange(BAND)[:, None]
    kj = np.arange(2 * BAND)[None, :]
    rel = qi + BAND - kj
    valid = (rel >= 0) & (rel <= BAND)
    bks = []
    for g in range(A_GROUPS):
        bk = np.where(valid, _t5_bucket_np(np.clip(rel, 0, BAND) * A_DILATIONS[g]), -1)
        bks.append(np.stack([bk, np.where(kj < BAND, -1, bk)]))
    bk_all = jnp.asarray(np.stack(bks).astype(np.int32))
    return pl.pallas_call(
        _bias_p_kernel,
        grid=(A_GROUPS, 2, A_HEADS),
        in_specs=[
            pl.BlockSpec(memory_space=pltpu.SMEM),
            pl.BlockSpec((1, 1, BAND, 2 * BAND), lambda g, f, h: (g, f, 0, 0)),
        ],
        out_specs=pl.BlockSpec((1, 1, 1, BAND, 2 * BAND), lambda g, f, h: (g, f, h, 0, 0)),
        out_shape=jax.ShapeDtypeStruct((A_GROUPS, 2, A_HEADS, BAND, 2 * BAND), F32),
        name="bias_prompt",
    )(rel_bias.astype(F32), bk_all)


def _bias_s_kernel(rb_ref, bk_ref, o_ref, *, g):
    h = pl.program_id(0)
    bk = bk_ref[...]
    acc = jnp.full(bk.shape, NEG, F32)
    for b in range(NUM_BUCKETS):
        acc = jnp.where(bk == b, rb_ref[b, g * A_HEADS + h], acc)
    o_ref[0] = acc


def _bias_s_call(rel_bias, g, lw):
    dil = A_DILATIONS[g]
    t = np.arange(DEC_ROWS)[:, None]
    col = np.arange(lw + LANES)[None, :]
    dist = lw + t - col
    ok = (dist >= 0) & (dist % dil == 0) & (dist // dil <= BAND) & (col < lw + DEC_ROWS)
    bk = np.where(ok, _t5_bucket_np(np.maximum(dist, 0)), -1).astype(np.int32)
    return pl.pallas_call(
        functools.partial(_bias_s_kernel, g=g),
        grid=(A_HEADS,),
        in_specs=[
            pl.BlockSpec(memory_space=pltpu.SMEM),
            pl.BlockSpec((DEC_ROWS, lw + LANES), lambda h: (0, 0)),
        ],
        out_specs=pl.BlockSpec((1, DEC_ROWS, lw + LANES), lambda h: (h, 0, 0)),
        out_shape=jax.ShapeDtypeStruct((A_HEADS, DEC_ROWS, lw + LANES), F32),
        name=f"bias_sample_g{g}",
    )(rel_bias.astype(F32), jnp.asarray(bk))


def _ln_mm_kernel(x_ref, mod_ref, w_ref, *refs, ranges):
    out_refs = refs[:len(ranges)]
    h_ref = refs[len(ranges)]
    j = pl.program_id(2)

    @pl.when(j == 0)
    def _():
        h_ref[...] = _modulated(x_ref, mod_ref, 0)

    acc = _dot(h_ref[...], w_ref[...])
    for (start, cnt, _), o_ref in zip(ranges, out_refs):
        @pl.when((j >= start) & (j < start + cnt))
        def _(o_ref=o_ref):
            o_ref[0] = acc.astype(o_ref.dtype)


def _ln_mm_call(x, mod, w, *, ranges, tm, tn, name):
    bsz, s, d = x.shape
    nj = w.shape[1] // tn
    assert sum(r[1] for r in ranges) == nj
    out_specs = [
        pl.BlockSpec((1, tm, tn), lambda b, i, j, st=st, c=c: (b, i, jnp.clip(j - st, 0, c - 1)))
        for (st, c, _) in ranges
    ]
    out_shape = [jax.ShapeDtypeStruct((bsz, s, c * tn), dt) for (_, c, dt) in ranges]
    return pl.pallas_call(
        functools.partial(_ln_mm_kernel, ranges=ranges),
        grid=(bsz, s // tm, nj),
        in_specs=[
            pl.BlockSpec((1, tm, d), lambda b, i, j: (b, i, 0)),
            _mod_spec(mod, tm, True),
            pl.BlockSpec((d, tn), lambda b, i, j: (0, j)),
        ],
        out_specs=out_specs,
        out_shape=out_shape,
        scratch_shapes=[pltpu.VMEM((tm, d), BF16)],
        compiler_params=_cparams(("parallel", "parallel", "arbitrary")),
        name=name,
    )(x, mod, w)


def _qkv_p_kernel(x_ref, mod_ref, w_ref, o0_ref, o1_ref, o2_ref, h_ref, scr_ref, *, tm):
    j = pl.program_id(2)

    @pl.when(j == 0)
    def _():
        h_ref[...] = _modulated(x_ref, mod_ref, 0)

    h = h_ref[...]
    scale = jnp.where(j == 0, QK_SCALE, 1.0).astype(F32)
    for g, o_ref in enumerate((o0_ref, o1_ref, o2_ref)):
        dil = A_DILATIONS[g]
        acc = _dot(h, w_ref[:, g * A_GROUP_WIDTH:(g + 1) * A_GROUP_WIDTH]) * scale
        if dil == 1:
            o_ref[0, 0] = acc.astype(BF16)
            continue
        for lc in range(LANE_SLABS):
            scr_ref[lc] = acc[:, lc * LANES:(lc + 1) * LANES]
        rows = tm // dil
        for r in range(dil):
            o_ref[0, r] = jnp.concatenate(
                [scr_ref[lc, pl.ds(r, rows, stride=dil), :] for lc in range(LANE_SLABS)], axis=1).astype(BF16)


def _qkv_p_call(x, mod, w_qkv, *, tm):
    bsz, s, d = x.shape
    out_specs, out_shape = [], []
    for g in range(A_GROUPS):
        dil = A_DILATIONS[g]
        out_specs.append(pl.BlockSpec((1, dil, tm // dil, A_GROUP_WIDTH), lambda b, i, j: (b, 0, i, j)))
        out_shape.append(jax.ShapeDtypeStruct((bsz, dil, s // dil, 3 * A_GROUP_WIDTH), BF16))
    return pl.pallas_call(
        functools.partial(_qkv_p_kernel, tm=tm),
        grid=(bsz, s // tm, 3),
        in_specs=[
            pl.BlockSpec((1, tm, d), lambda b, i, j: (b, i, 0)),
            _mod_spec(mod, tm, True),
            pl.BlockSpec((d, A_QKV_WIDTH), lambda b, i, j: (0, j)),
        ],
        out_specs=out_specs,
        out_shape=out_shape,
        scratch_shapes=[pltpu.VMEM((tm, d), BF16), pltpu.VMEM((LANE_SLABS, tm, LANES), F32)],
        compiler_params=_cparams(("parallel", "parallel", "arbitrary")),
        name="qkv_prompt",
    )(x, mod, w_qkv)


def _kv_tail_kernel(x_ref, mod_ref, w_ref, o0_ref, o1_ref, o2_ref, acc_ref, *, tmk):
    i = pl.program_id(1)
    last = pl.num_programs(1) - 1
    h = _modulated(x_ref, mod_ref, 0)

    def kv_t(rows, kv, g):
        c0 = (kv * A_GROUPS + g) * A_GROUP_WIDTH
        n = rows.shape[0]
        acc_ref[0:n, :] = _dot(rows, w_ref[:, c0:c0 + A_GROUP_WIDTH])
        return acc_ref[0:n, :].T

    for kv in range(2):
        o2_ref[0, kv] = kv_t(h, kv, 2)

    @pl.when(i == last)
    def _():
        for kv in range(2):
            o1_ref[0, kv] = kv_t(h, kv, 1)
            o0_ref[0, kv] = kv_t(h[tmk - A_WINDOWS[0]:, :], kv, 0)


def _kv_tail_call(x, mod, w_kv):
    bsz, s, d = x.shape
    tmk = A_WINDOWS[1]
    span = A_WINDOWS[2]
    nt = span // tmk
    off = (s - span) // tmk
    return pl.pallas_call(
        functools.partial(_kv_tail_kernel, tmk=tmk),
        grid=(bsz, nt),
        in_specs=[
            pl.BlockSpec((1, tmk, d), lambda b, i: (b, i + off, 0)),
            _mod_spec(mod, tmk, False),
            pl.BlockSpec((d, 2 * A_QKV_WIDTH), lambda b, i: (0, 0)),
        ],
        out_specs=[
            pl.BlockSpec((1, 2, A_GROUP_WIDTH, A_WINDOWS[0]), lambda b, i: (b, 0, 0, 0)),
            pl.BlockSpec((1, 2, A_GROUP_WIDTH, A_WINDOWS[1]), lambda b, i: (b, 0, 0, 0)),
            pl.BlockSpec((1, 2, A_GROUP_WIDTH, tmk), lambda b, i: (b, 0, 0, i)),
        ],
        out_shape=[jax.ShapeDtypeStruct((bsz, 2, A_GROUP_WIDTH, w), F32) for w in A_WINDOWS],
        scratch_shapes=[pltpu.VMEM((tmk, A_GROUP_WIDTH), F32)],
        compiler_params=_cparams(("parallel", "arbitrary")),
        name="kv_tail_prompt",
    )(x, mod, w_kv)


def _attn_p_kernel(q_ref, kp_ref, ko_ref, vp_ref, vo_ref, bias_ref, o_ref, l_ref, *, qb):
    n = pl.program_id(2)
    kall = jnp.concatenate([kp_ref[0, 0], ko_ref[0, 0]], axis=0)
    vall = jnp.concatenate([vp_ref[0, 0], vo_ref[0, 0]], axis=0)
    lane = lax.broadcasted_iota(jnp.int32, (BAND, LANES), 1)
    lo = lane < A_HEAD_DIM
    first = (n == 0).astype(jnp.int32)
    for j in range(qb // BAND):
        q = q_ref[0, 0, j * BAND:(j + 1) * BAND, :]
        kc = kall[j * BAND:(j + 2) * BAND]
        vc = vall[j * BAND:(j + 2) * BAND]
        variant = first if j == 0 else 0
        o_tiles = []
        lse_tile = jnp.zeros((BAND, LANES), F32)
        for hp in range(HEAD_PAIRS):
            cs = slice(hp * LANES, (hp + 1) * LANES)
            qp, kp, vp = q[:, cs], kc[:, cs], vc[:, cs]
            res = []
            for half in range(2):
                h = 2 * hp + half
                qm = jnp.where(lo if half == 0 else jnp.logical_not(lo), qp, jnp.zeros_like(qp))
                s = _dot_nt(qm, kp) + bias_ref[variant, h]
                m = jnp.max(s, axis=-1, keepdims=True)
                p = jnp.exp(s - m)
                l = jnp.sum(p, axis=-1, keepdims=True)
                res.append(_dot(p.astype(BF16), vp) * (1.0 / l))
                lse_tile = jnp.where(lane == h, m + jnp.log(l), lse_tile)
            o_tiles.append(jnp.where(lo, res[0], res[1]))
        o_ref[0, 0, j * BAND:(j + 1) * BAND, :] = jnp.concatenate(o_tiles, axis=1).astype(o_ref.dtype)
        l_ref[0, 0, j * BAND:(j + 1) * BAND, :] = lse_tile


def _attn_p_call(qkv_g, bias_p, g):
    bsz, dil, m, _ = qkv_g.shape
    qb = min(ATTN_QB, m)
    sub = qb // BAND
    own = lambda which: pl.BlockSpec((1, 1, qb, A_GROUP_WIDTH), lambda b, r, n: (b, r, n, which))
    prev = lambda which: pl.BlockSpec((1, 1, BAND, A_GROUP_WIDTH),
                                      lambda b, r, n: (b, r, jnp.maximum(n * sub - 1, 0), which))
    return pl.pallas_call(
        functools.partial(_attn_p_kernel, qb=qb),
        grid=(bsz, dil, m // qb),
        in_specs=[
            own(0), prev(1), own(1), prev(2), own(2),
            pl.BlockSpec((None, 2, A_HEADS, BAND, 2 * BAND), lambda b, r, n: (g, 0, 0, 0, 0)),
        ],
        out_specs=[
            pl.BlockSpec((1, 1, qb, A_GROUP_WIDTH), lambda b, r, n: (b, r, n, 0)),
            pl.BlockSpec((1, 1, qb, LANES), lambda b, r, n: (b, r, n, 0)),
        ],
        out_shape=[
            jax.ShapeDtypeStruct((bsz, dil, m, A_GROUP_WIDTH), BF16),
            jax.ShapeDtypeStruct((bsz, dil, m, LANES), F32),
        ],
        compiler_params=_cparams(("parallel", "parallel", "arbitrary")),
        name=f"attn_prompt_g{g}",
    )(qkv_g, qkv_g, qkv_g, qkv_g, qkv_g, bias_p)


def _attn_s_kernel(q_ref, new_ref, cache_ref, bias_ref, *refs, lw, aliased):
    if aliased:
        refs = refs[1:]
    out_ref, o_ref, lse_ref, pc_ref, pn_ref = refs
    ph = pl.program_id(1)
    new_t = jnp.concatenate([new_ref[0], jnp.zeros((LANES - DEC_ROWS, A_GROUP_WIDTH), F32)], axis=0).T
    lane = lax.broadcasted_iota(jnp.int32, (A_HEAD_DIM, LANES), 1)
    for h in range(A_HEADS):
        rolled = pltpu.roll(cache_ref[h], lw - DEC_ROWS, 1)
        tail = pltpu.roll(new_t[h * A_HEAD_DIM:(h + 1) * A_HEAD_DIM, :], LANES - DEC_ROWS, 1)
        if lw > LANES:
            out_ref[h, :, 0:lw - LANES] = rolled[:, 0:lw - LANES]
        out_ref[h, :, lw - LANES:lw] = jnp.where(lane >= LANES - DEC_ROWS, tail, rolled[:, lw - LANES:lw])

    lane8 = lax.broadcasted_iota(jnp.int32, (DEC_ROWS, LANES), 1)
    lo8 = lane8 < A_HEAD_DIM
    lane16 = lax.broadcasted_iota(jnp.int32, (PAD_ROWS, LANES), 1)
    lo16 = lane16 < A_HEAD_DIM
    zpad = lambda a: jnp.concatenate([a, jnp.zeros((PAD_ROWS - DEC_ROWS, a.shape[1]), F32)], axis=0)

    @pl.when(ph == 0)
    def _():
        q = zpad(q_ref[0] * QK_SCALE)
        lse_c = jnp.zeros((DEC_ROWS, LANES), F32)
        for hp in range(HEAD_PAIRS):
            qp = q[:, hp * LANES:(hp + 1) * LANES]
            kp = cache_ref[2 * hp:2 * hp + 2].reshape(LANES, lw).astype(BF16)
            kn = new_t[hp * LANES:(hp + 1) * LANES, :].astype(BF16)
            for half in range(2):
                h = 2 * hp + half
                qm = jnp.where(lo16 if half == 0 else jnp.logical_not(lo16), qp, 0.0).astype(BF16)
                s_c = _dot(qm, kp)[:DEC_ROWS] + bias_ref[h, :, 0:lw]
                s_n = _dot(qm, kn)[:DEC_ROWS] + bias_ref[h, :, lw:lw + LANES]
                m = jnp.maximum(jnp.max(s_c, axis=-1, keepdims=True), jnp.max(s_n, axis=-1, keepdims=True))
                p_c = jnp.exp(s_c - m)
                p_n = jnp.exp(s_n - m)
                l = jnp.sum(p_c, axis=-1, keepdims=True) + jnp.sum(p_n, axis=-1, keepdims=True)
                inv = 1.0 / l
                pc_ref[h] = zpad(p_c * inv)
                pn_ref[h] = zpad(p_n * inv)
                lse_c = jnp.where(lane8 == h, m + jnp.log(l), lse_c)
        lse_ref[0] = lse_c

    @pl.when(ph == 1)
    def _():
        o_tiles = []
        for hp in range(HEAD_PAIRS):
            vp = cache_ref[2 * hp:2 * hp + 2].reshape(LANES, lw).astype(BF16)
            vn = new_t[hp * LANES:(hp + 1) * LANES, :].astype(BF16)
            res = []
            for half in range(2):
                h = 2 * hp + half
                o = _dot_nt(pc_ref[h].astype(BF16), vp) + _dot_nt(pn_ref[h].astype(BF16), vn)
                res.append(o[:DEC_ROWS])
            o_tiles.append(jnp.where(lo8, res[0], res[1]))
        o_ref[0] = jnp.concatenate(o_tiles, axis=1)


def _attn_s_call(qkv_s, cache_t, prev_out, bias_s, layer, g):
    depth, bd, _, _, _, lw = cache_t.shape
    assert lw % LANES == 0
    aliased = prev_out is not None
    cache_blk = (None, None, None, A_HEADS, A_HEAD_DIM, lw)
    in_specs = [
        pl.BlockSpec((1, DEC_ROWS, A_GROUP_WIDTH), lambda b, ph: (b, 0, g)),
        pl.BlockSpec((1, DEC_ROWS, A_GROUP_WIDTH), lambda b, ph: (b, 0, A_GROUPS * (1 + ph) + g)),
        pl.BlockSpec(cache_blk, lambda b, ph: (layer, b, ph, 0, 0, 0)),
        pl.BlockSpec((A_HEADS, DEC_ROWS, lw + LANES), lambda b, ph: (0, 0, 0)),
    ]
    args = [qkv_s, qkv_s, cache_t, bias_s]
    aliases = {}
    if aliased:
        in_specs.append(pl.BlockSpec(memory_space=pl.ANY))
        args.append(prev_out)
        aliases = {4: 0}
    return pl.pallas_call(
        functools.partial(_attn_s_kernel, lw=lw, aliased=aliased),
        grid=(bd, 2),
        in_specs=in_specs,
        out_specs=[
            pl.BlockSpec(cache_blk, lambda b, ph: (layer, b, ph, 0, 0, 0)),
            pl.BlockSpec((1, DEC_ROWS, A_GROUP_WIDTH), lambda b, ph: (b, 0, 0)),
            pl.BlockSpec((1, DEC_ROWS, LANES), lambda b, ph: (b, 0, 0)),
        ],
        out_shape=[
            jax.ShapeDtypeStruct(cache_t.shape, F32),
            jax.ShapeDtypeStruct((bd, DEC_ROWS, A_GROUP_WIDTH), F32),
            jax.ShapeDtypeStruct((bd, DEC_ROWS, LANES), F32),
        ],
        scratch_shapes=[
            pltpu.VMEM((A_HEADS, PAD_ROWS, lw), F32),
            pltpu.VMEM((A_HEADS, PAD_ROWS, LANES), F32),
        ],
        input_output_aliases=aliases,
        compiler_params=_cparams(("parallel", "arbitrary")),
        name=f"attn_sample_g{g}",
    )(*args)


def _hgrn_head(q, k, g, v, st, tt):
    ti = lax.broadcasted_iota(jnp.int32, (tt, tt), 0)
    si = lax.broadcasted_iota(jnp.int32, (tt, tt), 1)
    row = lax.broadcasted_iota(jnp.int32, (tt, B_DIM), 0)
    a = jnp.where(ti == si, _dot_nt(q.astype(BF16), k.astype(BF16)), 0.0)
    c = g
    r = jnp.zeros_like(g)
    lg = 0
    while (1 << lg) < tt:
        b = 1 << lg
        ab = _dot_nt((q * jnp.exp(c)).astype(BF16), (k * jnp.exp(r)).astype(BF16))
        pair = ((ti >> lg) == (si >> lg) + 1) & ((ti >> (lg + 1)) == (si >> (lg + 1)))
        a = jnp.where(pair, ab, a)
        tot = c + r
        second = ((row >> lg) & 1) == 1
        c = c + jnp.where(second, pltpu.roll(tot, b, 0), 0.0)
        r = r + jnp.where(second, 0.0, pltpu.roll(tot, tt - b, 0))
        lg += 1
    o = _dot(a.astype(BF16), v.astype(BF16)) + _dot_nt((q * jnp.exp(c)).astype(BF16), st.astype(BF16))
    dec = jnp.exp(c[tt - 1:tt, :])
    st_new = st * dec + _dot(v.T.astype(BF16), (k * jnp.exp(r)).astype(BF16))
    return o, st_new


def _hgrn_kernel(qb_ref, fb_ref, ib_ref, gb_ref, lb_ref, nw_ref, *refs, tt, valid, has_s0):
    if has_s0:
        s0_ref, y_ref, s_out_ref, st_ref = refs
    else:
        y_ref, s_out_ref, st_ref = refs
    t = pl.program_id(1)
    nt = pl.num_programs(1)

    @pl.when(t == 0)
    def _():
        for hh in range(B_HEADS):
            if has_s0:
                st_ref[hh] = s0_ref[hh].T
            else:
                st_ref[hh] = jnp.zeros((B_DIM, B_DIM), F32)

    def load(ref):
        x = ref[0]
        if valid < tt:
            x = jnp.concatenate([x, jnp.zeros((tt - valid, x.shape[1]), F32)], axis=0)
        return x

    qb, fb, ib, gb = load(qb_ref), load(fb_ref), load(ib_ref), load(gb_ref)
    live = lax.broadcasted_iota(jnp.int32, (tt, B_DIM), 0) < valid
    for hh in range(B_HEADS):
        cs = slice(hh * B_DIM, (hh + 1) * B_DIM)
        lb = lb_ref[:, cs]
        sig = _sigmoid(fb[:, cs])
        g = jnp.log(jnp.maximum(lb + (1.0 - lb) * sig, TINY))
        k = (1.0 - lb) * (1.0 - sig)
        if valid < tt:
            g = jnp.where(live, g, 0.0)
            k = jnp.where(live, k, 0.0)
        o, st_new = _hgrn_head(_silu(qb[:, cs]), k, g, ib[:, cs], st_ref[hh], tt)
        st_ref[hh] = st_new
        o = o * lax.rsqrt(jnp.mean(o * o, axis=-1, keepdims=True) + RMS_EPS) * nw_ref[:, cs]
        y = o * _silu(gb[:, cs])
        y_ref[0, :, cs] = y[:valid].astype(y_ref.dtype)

    @pl.when(t == nt - 1)
    def _():
        for hh in range(B_HEADS):
            s_out_ref[0, hh] = st_ref[hh].T


def _hgrn_call(hg, lb, nw, s0, *, layer, seq_rows, name):
    bsz, s, _ = hg.shape
    tt = HGRN_TILE
    valid = min(seq_rows, tt)
    nt = s // valid
    has_s0 = s0 is not None
    blk = (1, valid, B_WIDTH)
    in_specs = [pl.BlockSpec(blk, lambda b, t, c=c: (b, t, c)) for c in range(4)]
    in_specs += [pl.BlockSpec((1, B_WIDTH), lambda b, t: (0, 0))] * 2
    args = [hg, hg, hg, hg, lb, nw]
    if has_s0:
        in_specs.append(pl.BlockSpec((None, None, B_HEADS, B_DIM, B_DIM), lambda b, t: (layer, b, 0, 0, 0)))
        args.append(s0)
    return pl.pallas_call(
        functools.partial(_hgrn_kernel, tt=tt, valid=valid, has_s0=has_s0),
        grid=(bsz, nt),
        in_specs=in_specs,
        out_specs=[
            pl.BlockSpec(blk, lambda b, t: (b, t, 0)),
            pl.BlockSpec((1, B_HEADS, B_DIM, B_DIM), lambda b, t: (b, 0, 0, 0)),
        ],
        out_shape=[
            jax.ShapeDtypeStruct((bsz, s, B_WIDTH), BF16),
            jax.ShapeDtypeStruct((bsz, B_HEADS, B_DIM, B_DIM), F32),
        ],
        scratch_shapes=[pltpu.VMEM((B_HEADS, B_DIM, B_DIM), F32)],
        compiler_params=_cparams(("parallel", "arbitrary")),
        name=name,
    )(*args)


def _token_major(ref, dil, tm, scr_ref):
    if dil == 1:
        return ref[0, 0].astype(F32)
    rows = tm // dil
    slabs = ref.shape[3] // LANES
    for r in range(dil):
        blk = ref[0, r].astype(F32)
        for lc in range(slabs):
            scr_ref[lc, pl.ds(r, rows, stride=dil), :] = blk[:, lc * LANES:(lc + 1) * LANES]
    return jnp.concatenate([scr_ref[lc] for lc in range(slabs)], axis=1)


def _merge_kernel(o0_ref, o1_ref, o2_ref, l0_ref, l1_ref, l2_ref, yb_ref, gate_ref, x_ref, mod_ref,
                  wa_ref, wb_ref, wo_ref, g_ref, b_ref, out_ref, scr_ref, *, dils, tm):
    hi = lax.broadcasted_iota(jnp.int32, (LANES, A_GROUP_WIDTH), 0)
    ci = lax.broadcasted_iota(jnp.int32, (LANES, A_GROUP_WIDTH), 1)
    spread = jnp.where((ci >> 6) == hi, 1.0, 0.0).astype(BF16)

    def widen(lse):
        p0 = lse.astype(BF16)
        r0 = lse - p0.astype(F32)
        p1 = r0.astype(BF16)
        p2 = (r0 - p1.astype(F32)).astype(BF16)
        return _dot(p0, spread) + _dot(p1, spread) + _dot(p2, spread)

    os, ls = [], []
    for o_ref, l_ref, dil in zip((o0_ref, o1_ref, o2_ref), (l0_ref, l1_ref, l2_ref), dils):
        os.append(_token_major(o_ref, dil, tm, scr_ref))
        ls.append(widen(_token_major(l_ref, dil, tm, scr_ref)))
    m = jnp.maximum(jnp.maximum(ls[0], ls[1]), ls[2])
    es = [jnp.exp(l - m) for l in ls]
    ya = (es[0] * os[0] + es[1] * os[1] + es[2] * os[2]) / (es[0] + es[1] + es[2])
    br_a = _dot(ya.astype(BF16), wa_ref[...])
    br_b = _dot(yb_ref[0].astype(BF16), wb_ref[...])
    merged = _sigmoid(gate_ref[0, :, 0:D_MODEL]) * br_a + _sigmoid(gate_ref[0, :, D_MODEL:2 * D_MODEL]) * br_b
    mix = _dot(merged.astype(BF16), wo_ref[...])
    gt = mod_ref[0, :, 2 * D_MODEL:3 * D_MODEL]
    z = ALPHA * x_ref[0] + (1.0 + gt) * mix
    out_ref[0] = _ln(z) * g_ref[...] + b_ref[...]


def _merge_call(os, lses, yb, gates, x, mod, wa, wb, wo, ln_g, ln_b, *, tm, name):
    bsz, s, d = x.shape
    dils = tuple(o.shape[1] for o in os)
    tok = lambda w: pl.BlockSpec((1, tm, w), lambda b, i: (b, i, 0))
    res = lambda a: pl.BlockSpec((1, a.shape[1], tm // a.shape[1], a.shape[3]), lambda b, i: (b, 0, i, 0))
    full = lambda a: pl.BlockSpec(a.shape, lambda b, i: (0, 0))
    return pl.pallas_call(
        functools.partial(_merge_kernel, dils=dils, tm=tm),
        grid=(bsz, s // tm),
        in_specs=[res(a) for a in os] + [res(a) for a in lses] + [
            tok(B_WIDTH), tok(2 * D_MODEL), tok(d), _mod_spec(mod, tm, False),
            full(wa), full(wb), full(wo), full(ln_g), full(ln_b)],
        out_specs=tok(d),
        out_shape=jax.ShapeDtypeStruct((bsz, s, d), F32),
        scratch_shapes=[pltpu.VMEM((LANE_SLABS, tm, LANES), F32)],
        compiler_params=_cparams(("parallel", "parallel")),
        name=name,
    )(*os, *lses, yb, gates, x, mod, wa, wb, wo, ln_g, ln_b)


def _ffn_kernel(x_ref, mod_ref, wua_ref, wug_ref, wd_ref, g_ref, b_ref, out_ref, h_ref, acc_ref, act_ref):
    j = pl.program_id(2)
    last = pl.num_programs(2) - 1

    @pl.when(j == 0)
    def _():
        h_ref[...] = _modulated(x_ref, mod_ref, 3)

    h = h_ref[...]
    for st, sz in FF_SUBTILES:
        a = _dot(h, wua_ref[:, st:st + sz])
        gg = _dot(h, wug_ref[:, st:st + sz])
        act_ref[:, st:st + sz] = (_silu(a) * gg).astype(BF16)
    part = _dot(act_ref[...], wd_ref[...])

    @pl.when(j == 0)
    def _():
        acc_ref[...] = part

    @pl.when((j > 0) & (j < last))
    def _():
        acc_ref[...] += part

    @pl.when(j == last)
    def _():
        gt = mod_ref[0, :, 5 * D_MODEL:6 * D_MODEL]
        z = ALPHA * x_ref[0] + (1.0 + gt) * (acc_ref[...] + part)
        out_ref[0] = _ln(z) * g_ref[...] + b_ref[...]


def _ffn_call(x, mod, w_up, w_down, ln_g, ln_b, *, tm, name):
    bsz, s, d = x.shape
    nf = FF_DIM // FF_CHUNK
    assert nf >= 2 and sum(sz for _, sz in FF_SUBTILES) == FF_CHUNK
    return pl.pallas_call(
        _ffn_kernel,
        grid=(bsz, s // tm, nf),
        in_specs=[
            pl.BlockSpec((1, tm, d), lambda b, i, j: (b, i, 0)),
            _mod_spec(mod, tm, True),
            pl.BlockSpec((d, FF_CHUNK), lambda b, i, j: (0, j)),
            pl.BlockSpec((d, FF_CHUNK), lambda b, i, j: (0, j + nf)),
            pl.BlockSpec((FF_CHUNK, d), lambda b, i, j: (j, 0)),
            pl.BlockSpec((1, d), lambda b, i, j: (0, 0)),
            pl.BlockSpec((1, d), lambda b, i, j: (0, 0)),
        ],
        out_specs=pl.BlockSpec((1, tm, d), lambda b, i, j: (b, i, 0)),
        out_shape=jax.ShapeDtypeStruct((bsz, s, d), F32),
        scratch_shapes=[pltpu.VMEM((tm, d), BF16), pltpu.VMEM((tm, d), F32), pltpu.VMEM((tm, FF_CHUNK), BF16)],
        compiler_params=_cparams(("parallel", "parallel", "arbitrary")),
        name=name,
    )(x, mod, w_up, w_up, w_down, ln_g, ln_b)


def kernel(x_prompt, x_sample, c_prompt, c_sample, cache_kv_w128, cache_kv_w512, cache_kv_w2048, state_hgrn,
           w_in, w_branch_a, w_branch_b, w_out, hgrn_norm_w, hgrn_lb_logits, rel_bias,
           ffn_w_up, ffn_w_down, w_ada, b_ada, ln1_g, ln1_b, ln2_g, ln2_b):
    depth = w_in.shape[0]
    bp, sp, d = x_prompt.shape
    bd, sd, _ = x_sample.shape
    assert sd == DEC_ROWS and d == D_MODEL
    max_win = A_WINDOWS[-1]
    assert sp % max_win == 0
    tm_p = 1024
    tm_s = bd * sd
    n_qkv = 3 * A_QKV_WIDTH

    w_in_b = w_in.astype(BF16)
    w_qkv = w_in_b[:, :, :n_qkv]
    w_kv = w_in_b[:, :, A_QKV_WIDTH:n_qkv]
    w_rest = w_in_b[:, :, n_qkv:]
    wa_b = w_branch_a.astype(BF16)
    wb_b = w_branch_b.astype(BF16)
    wo_b = w_out.astype(BF16)
    wup_b = ffn_w_up.astype(BF16)
    wdn_b = ffn_w_down.astype(BF16)

    lbs = _lbs_call(hgrn_lb_logits)
    n_c = bp + bd
    n_c_pad = -(-n_c // 8) * 8
    c_all = jnp.concatenate([c_prompt, c_sample, jnp.zeros((n_c_pad - n_c, d), F32)], axis=0)
    mod_all = _ada_call(c_all, w_ada, b_ada)
    bias_p = _bias_p_call(rel_bias)

    caches_t = [c.transpose(0, 1, 2, 4, 5, 3) for c in (cache_kv_w128, cache_kv_w512, cache_kv_w2048)]
    bias_s = [_bias_s_call(rel_bias, g, caches_t[g].shape[-1]) for g in range(A_GROUPS)]
    new_caches = [None] * A_GROUPS
    kv_p = [[] for _ in range(A_GROUPS)]
    s_p, s_s = [], []

    xp = x_prompt
    xs = x_sample.reshape(1, tm_s, d)
    rest_ranges = ((0, 2, F32), (2, 2, F32))
    sample_ranges = ((0, 9, F32), (9, 4, F32), (13, 4, F32))
    for l in range(depth):
        mod_p = mod_all[l, 0:bp].reshape(bp, 1, N_MOD)
        mod_s = jnp.repeat(mod_all[l, bp:bp + bd], sd, axis=0).reshape(1, tm_s, N_MOD)
        lb = lbs[l:l + 1]
        nw = hgrn_norm_w[l:l + 1].astype(F32)
        g1, b1 = ln1_g[l:l + 1], ln1_b[l:l + 1]
        g2, b2 = ln2_g[l:l + 1], ln2_b[l:l + 1]

        qkv_groups = _qkv_p_call(xp, mod_p, w_qkv[l], tm=tm_p)
        hg, gates = _ln_mm_call(xp, mod_p, w_rest[l], ranges=rest_ranges, tm=tm_p, tn=1024, name="rest_proj_prompt")
        tails = _kv_tail_call(xp, mod_p, w_kv[l])
        os, lses = [], []
        for g in range(A_GROUPS):
            o, lse = _attn_p_call(qkv_groups[g], bias_p, g)
            os.append(o)
            lses.append(lse)
            kv_p[g].append(tails[g])
        yb, st = _hgrn_call(hg, lb, nw, None, layer=l, seq_rows=HGRN_TILE, name="hgrn_prompt")
        s_p.append(st)
        xp = _merge_call(os, lses, yb, gates, xp, mod_p, wa_b[l], wb_b[l], wo_b[l], g1, b1, tm=512,
                         name="merge_prompt")
        xp = _ffn_call(xp, mod_p, wup_b[l], wdn_b[l], g2, b2, tm=tm_p, name="ffn_prompt")

        qkv_s, hg_s, gates_s = _ln_mm_call(xs, mod_s, w_in_b[l], ranges=sample_ranges, tm=tm_s, tn=512,
                                           name="in_proj_sample")
        qkv_s3 = qkv_s.reshape(bd, sd, n_qkv)
        os, lses = [], []
        for g in range(A_GROUPS):
            new_caches[g], o, lse = _attn_s_call(qkv_s3, caches_t[g], new_caches[g], bias_s[g], l, g)
            os.append(o.reshape(1, 1, tm_s, A_GROUP_WIDTH))
            lses.append(lse.reshape(1, 1, tm_s, LANES))
        yb_s, st_s = _hgrn_call(hg_s.reshape(bd, sd, 4 * B_WIDTH), lb, nw, state_hgrn, layer=l, seq_rows=sd,
                                name="hgrn_sample")
        s_s.append(st_s)
        xs = _merge_call(os, lses, yb_s.reshape(1, tm_s, B_WIDTH), gates_s, xs, mod_s, wa_b[l], wb_b[l], wo_b[l],
                         g1, b1, tm=tm_s, name="merge_sample")
        xs = _ffn_call(xs, mod_s, wup_b[l], wdn_b[l], g2, b2, tm=tm_s, name="ffn_sample")

    def rows_last(a):
        return a.transpose(0, 1, 2, 5, 3, 4)

    outs_kv_p = [rows_last(jnp.stack(kv_p[g]).reshape(depth, bp, 2, A_HEADS, A_HEAD_DIM, -1))
                 for g in range(A_GROUPS)]
    outs_kv_s = [rows_last(nc) for nc in new_caches]
    return (xp, xs.reshape(bd, sd, d),
            outs_kv_p[0], outs_kv_p[1], outs_kv_p[2], jnp.stack(s_p),
            outs_kv_s[0], outs_kv_s[1], outs_kv_s[2], jnp.stack(s_s))
```

```python
import functools
import math

import jax
import jax.numpy as jnp
import numpy as np
from jax import lax
from jax.experimental import pallas as pl
from jax.experimental.pallas import tpu as pltpu

F32 = jnp.float32
BF16 = jnp.bfloat16

D_MODEL = 1024
A_WINDOWS = (128, 512, 2048)
A_DILATIONS = (1, 4, 16)
A_GROUPS = 3
A_HEADS = 8
A_HEAD_DIM = 64
A_GROUP_WIDTH = A_HEADS * A_HEAD_DIM
A_QKV_WIDTH = A_GROUPS * A_GROUP_WIDTH
BAND = 128
NUM_BUCKETS = 32
MAX_DISTANCE = 2048
B_HEADS = 4
B_DIM = 128
B_WIDTH = B_HEADS * B_DIM
FF_DIM = 2816
N_MOD = 6 * D_MODEL
DEPTH = 4
ALPHA = (2 * DEPTH) ** 0.25
LN_EPS = 1e-5
RMS_EPS = 1e-6
NEG = -1e30
TINY = 1e-30
QK_SCALE = A_HEAD_DIM ** -0.5

LANES = 128
HEAD_PAIRS = A_GROUP_WIDTH // LANES
LANE_SLABS = A_GROUP_WIDTH // LANES
FF_SUBTILES = ((0, 512), (512, 512), (1024, 512), (1536, 512), (2048, 512), (2560, 256))
HGRN_TILE = 128
ATTN_QB = 512
DEC_ROWS = 8
PAD_ROWS = 16
VMEM_LIMIT = 56 * 1024 * 1024


def _cparams(sem):
    return pltpu.CompilerParams(dimension_semantics=sem, vmem_limit_bytes=VMEM_LIMIT)


def _sigmoid(x):
    return 1.0 / (1.0 + jnp.exp(-x))


def _silu(x):
    return x * _sigmoid(x)


def _ln(x):
    mu = jnp.mean(x, axis=-1, keepdims=True)
    xc = x - mu
    var = jnp.mean(xc * xc, axis=-1, keepdims=True)
    return xc * lax.rsqrt(var + LN_EPS)


def _dot(a, b):
    return jnp.dot(a, b, preferred_element_type=F32)


def _dot_nt(a, b):
    return lax.dot_general(a, b, (((1,), (1,)), ((), ())), preferred_element_type=F32)


def _modulated(x_ref, mod_ref, shift_chunk):
    sh = mod_ref[0, :, shift_chunk * D_MODEL:(shift_chunk + 1) * D_MODEL]
    sc = mod_ref[0, :, (shift_chunk + 1) * D_MODEL:(shift_chunk + 2) * D_MODEL]
    return (_ln(x_ref[0]) * (1.0 + sc) + sh).astype(BF16)


def _mod_spec(mod, tm, rank3):
    per_row = mod.shape[1] != 1
    if rank3:
        if per_row:
            return pl.BlockSpec((1, tm, N_MOD), lambda b, i, j: (b, i, 0))
        return pl.BlockSpec((1, 1, N_MOD), lambda b, i, j: (b, 0, 0))
    if per_row:
        return pl.BlockSpec((1, tm, N_MOD), lambda b, i: (b, i, 0))
    return pl.BlockSpec((1, 1, N_MOD), lambda b, i: (b, 0, 0))


def _lbs_kernel(lg_ref, out_ref):
    x = lg_ref[...]
    m = jnp.max(x, axis=0, keepdims=True)
    e = jnp.exp(x - m)
    p = e / jnp.sum(e, axis=0, keepdims=True)
    rows = []
    c = None
    for l in range(x.shape[0]):
        c = p[l:l + 1] if c is None else c + p[l:l + 1]
        rows.append(c - p[0:1])
    out_ref[...] = jnp.concatenate(rows, axis=0)


def _lbs_call(logits):
    return pl.pallas_call(
        _lbs_kernel, out_shape=jax.ShapeDtypeStruct(logits.shape, F32), name="hgrn_lbs",
    )(logits.astype(F32))


def _ada_kernel(c_ref, w_ref, b_ref, o_ref):
    o_ref[0] = _dot(c_ref[...].astype(BF16), w_ref[0].astype(BF16)) + b_ref[0]


def _ada_call(c_all, w_ada, b_ada):
    depth = w_ada.shape[0]
    rows = c_all.shape[0]
    tn = 512
    return pl.pallas_call(
        _ada_kernel,
        grid=(depth, N_MOD // tn),
        in_specs=[
            pl.BlockSpec((rows, D_MODEL), lambda l, j: (0, 0)),
            pl.BlockSpec((1, D_MODEL, tn), lambda l, j: (l, 0, j)),
            pl.BlockSpec((1, 1, tn), lambda l, j: (l, 0, j)),
        ],
        out_specs=pl.BlockSpec((1, rows, tn), lambda l, j: (l, 0, j)),
        out_shape=jax.ShapeDtypeStruct((depth, rows, N_MOD), F32),
        compiler_params=_cparams(("parallel", "parallel")),
        name="ada_mod",
    )(c_all, w_ada, b_ada.reshape(depth, 1, N_MOD))


def _t5_bucket_np(n):
    n = np.asarray(n, np.int32)
    max_exact = NUM_BUCKETS // 2
    nf = np.maximum(n.astype(np.float32), np.float32(max_exact))
    val = (np.log(nf / np.float32(max_exact)) / np.float32(math.log(MAX_DISTANCE / max_exact))
           * np.float32(NUM_BUCKETS - max_exact))
    large = max_exact + val.astype(np.int32)
    large = np.minimum(large, NUM_BUCKETS - 1)
    return np.where(n < max_exact, n, large).astype(np.int32)


def _bias_p_kernel(rb_ref, bk_ref, o_ref):
    g = pl.program_id(0)
    h = pl.program_id(2)
    bk = bk_ref[0, 0]
    acc = jnp.full(bk.shape, NEG, F32)
    for b in range(NUM_BUCKETS):
        acc = jnp.where(bk == b, rb_ref[b, g * A_HEADS + h], acc)
    o_ref[0, 0, 0] = acc


def _bias_p_call(rel_bias):
    qi = np.arange(BAND)[:, None]
    kj = np.arange(2 * BAND)[None, :]
    rel = qi + BAND - kj
    valid = (rel >= 0) & (rel <= BAND)
    bks = []
    for g in range(A_GROUPS):
        bk = np.where(valid, _t5_bucket_np(np.clip(rel, 0, BAND) * A_DILATIONS[g]), -1)
        bks.append(np.stack([bk, np.where(kj < BAND, -1, bk)]))
    bk_all = jnp.asarray(np.stack(bks).astype(np.int32))
    return pl.pallas_call(
        _bias_p_kernel,
        grid=(A_GROUPS, 2, A_HEADS),
        in_specs=[
            pl.BlockSpec(memory_space=pltpu.SMEM),
            pl.BlockSpec((1, 1, BAND, 2 * BAND), lambda g, f, h: (g, f, 0, 0)),
        ],
        out_specs=pl.BlockSpec((1, 1, 1, BAND, 2 * BAND), lambda g, f, h: (g, f, h, 0, 0)),
        out_shape=jax.ShapeDtypeStruct((A_GROUPS, 2, A_HEADS, BAND, 2 * BAND), F32),
        name="bias_prompt",
    )(rel_bias.astype(F32), bk_all)


def _bias_s_kernel(rb_ref, bk_ref, o_ref, *, g):
    h = pl.program_id(0)
    bk = bk_ref[...]
    acc = jnp.full(bk.shape, NEG, F32)
    for b in range(NUM_BUCKETS):
        acc = jnp.where(bk == b, rb_ref[b, g * A_HEADS + h], acc)
    o_ref[0] = acc


def _bias_s_call(rel_bias, g, lw):
    dil = A_DILATIONS[g]
    t = np.arange(DEC_ROWS)[:, None]
    col = np.arange(lw + LANES)[None, :]
    dist = lw + t - col
    ok = (dist >= 0) & (dist % dil == 0) & (dist // dil <= BAND) & (col < lw + DEC_ROWS)
    bk = np.where(ok, _t5_bucket_np(np.maximum(dist, 0)), -1).astype(np.int32)
    return pl.pallas_call(
        functools.partial(_bias_s_kernel, g=g),
        grid=(A_HEADS,),
        in_specs=[
            pl.BlockSpec(memory_space=pltpu.SMEM),
            pl.BlockSpec((DEC_ROWS, lw + LANES), lambda h: (0, 0)),
        ],
        out_specs=pl.BlockSpec((1, DEC_ROWS, lw + LANES), lambda h: (h, 0, 0)),
        out_shape=jax.ShapeDtypeStruct((A_HEADS, DEC_ROWS, lw + LANES), F32),
        name=f"bias_sample_g{g}",
    )(rel_bias.astype(F32), jnp.asarray(bk))


def _ln_mm_kernel(x_ref, mod_ref, w_ref, *refs, ranges):
    out_refs = refs[:len(ranges)]
    h_ref = refs[len(ranges)]
    j = pl.program_id(2)

    @pl.when(j == 0)
    def _():
        h_ref[...] = _modulated(x_ref, mod_ref, 0)

    for (start, cnt, _), o_ref in zip(ranges, out_refs):
        @pl.when((j >= start) & (j < start + cnt))
        def _(o_ref=o_ref):
            o_ref[0] = _dot(h_ref[...], w_ref[...]).astype(o_ref.dtype)


def _ln_mm_call(x, mod, w, layer, *, ranges, tm, tn, name):
    bsz, s, d = x.shape
    nj = w.shape[2] // tn
    assert sum(r[1] for r in ranges) == nj
    out_specs = [
        pl.BlockSpec((1, tm, tn), lambda b, i, j, st=st, c=c: (b, i, jnp.clip(j - st, 0, c - 1)))
        for (st, c, _) in ranges
    ]
    out_shape = [jax.ShapeDtypeStruct((bsz, s, c * tn), dt) for (_, c, dt) in ranges]
    return pl.pallas_call(
        functools.partial(_ln_mm_kernel, ranges=ranges),
        grid=(bsz, s // tm, nj),
        in_specs=[
            pl.BlockSpec((1, tm, d), lambda b, i, j: (b, i, 0)),
            _mod_spec(mod, tm, True),
            pl.BlockSpec((None, d, tn), lambda b, i, j: (layer, 0, j)),
        ],
        out_specs=out_specs,
        out_shape=out_shape,
        scratch_shapes=[pltpu.VMEM((tm, d), BF16)],
        compiler_params=_cparams(("parallel", "parallel", "arbitrary")),
        name=name,
    )(x, mod, w)


def _qkv_p_kernel(x_ref, mod_ref, w_ref, o0_ref, o1_ref, o2_ref, h_ref, scr_ref, *, tm):
    j = pl.program_id(2)

    @pl.when(j == 0)
    def _():
        h_ref[...] = _modulated(x_ref, mod_ref, 0)

    h = h_ref[...]
    scale = jnp.where(j == 0, QK_SCALE, 1.0).astype(F32)
    for g, o_ref in enumerate((o0_ref, o1_ref, o2_ref)):
        dil = A_DILATIONS[g]
        acc = _dot(h, w_ref[:, g * A_GROUP_WIDTH:(g + 1) * A_GROUP_WIDTH]) * scale
        if dil == 1:
            o_ref[0, 0] = acc.astype(BF16)
            continue
        for lc in range(LANE_SLABS):
            scr_ref[lc] = acc[:, lc * LANES:(lc + 1) * LANES]
        rows = tm // dil
        for r in range(dil):
            o_ref[0, r] = jnp.concatenate(
                [scr_ref[lc, pl.ds(r, rows, stride=dil), :] for lc in range(LANE_SLABS)], axis=1).astype(BF16)


def _qkv_p_call(x, mod, w_in, layer, *, tm):
    bsz, s, d = x.shape
    out_specs, out_shape = [], []
    for g in range(A_GROUPS):
        dil = A_DILATIONS[g]
        out_specs.append(pl.BlockSpec((1, dil, tm // dil, A_GROUP_WIDTH), lambda b, i, j: (b, 0, i, j)))
        out_shape.append(jax.ShapeDtypeStruct((bsz, dil, s // dil, 3 * A_GROUP_WIDTH), BF16))
    return pl.pallas_call(
        functools.partial(_qkv_p_kernel, tm=tm),
        grid=(bsz, s // tm, 3),
        in_specs=[
            pl.BlockSpec((1, tm, d), lambda b, i, j: (b, i, 0)),
            _mod_spec(mod, tm, True),
            pl.BlockSpec((None, d, A_QKV_WIDTH), lambda b, i, j: (layer, 0, j)),
        ],
        out_specs=out_specs,
        out_shape=out_shape,
        scratch_shapes=[pltpu.VMEM((tm, d), BF16), pltpu.VMEM((LANE_SLABS, tm, LANES), F32)],
        compiler_params=_cparams(("parallel", "parallel", "arbitrary")),
        name="qkv_prompt",
    )(x, mod, w_in)


def _kv_tail_kernel(x_ref, mod_ref, wk_ref, wv_ref, o0_ref, o1_ref, o2_ref, acc_ref, *, tmk):
    i = pl.program_id(1)
    last = pl.num_programs(1) - 1
    h = _modulated(x_ref, mod_ref, 0)

    def kv_t(rows, kv, g):
        w_ref = wv_ref if kv else wk_ref
        n = rows.shape[0]
        acc_ref[0:n, :] = _dot(rows, w_ref[:, g * A_GROUP_WIDTH:(g + 1) * A_GROUP_WIDTH])
        return acc_ref[0:n, :].T

    for kv in range(2):
        o2_ref[0, kv] = kv_t(h, kv, 2)

    @pl.when(i == last)
    def _():
        for kv in range(2):
            o1_ref[0, kv] = kv_t(h, kv, 1)
            o0_ref[0, kv] = kv_t(h[tmk - A_WINDOWS[0]:, :], kv, 0)


def _kv_tail_call(x, mod, w_in, layer):
    bsz, s, d = x.shape
    tmk = A_WINDOWS[1]
    span = A_WINDOWS[2]
    nt = span // tmk
    off = (s - span) // tmk
    return pl.pallas_call(
        functools.partial(_kv_tail_kernel, tmk=tmk),
        grid=(bsz, nt),
        in_specs=[
            pl.BlockSpec((1, tmk, d), lambda b, i: (b, i + off, 0)),
            _mod_spec(mod, tmk, False),
            pl.BlockSpec((None, d, A_QKV_WIDTH), lambda b, i: (layer, 0, 1)),
            pl.BlockSpec((None, d, A_QKV_WIDTH), lambda b, i: (layer, 0, 2)),
        ],
        out_specs=[
            pl.BlockSpec((1, 2, A_GROUP_WIDTH, A_WINDOWS[0]), lambda b, i: (b, 0, 0, 0)),
            pl.BlockSpec((1, 2, A_GROUP_WIDTH, A_WINDOWS[1]), lambda b, i: (b, 0, 0, 0)),
            pl.BlockSpec((1, 2, A_GROUP_WIDTH, tmk), lambda b, i: (b, 0, 0, i)),
        ],
        out_shape=[jax.ShapeDtypeStruct((bsz, 2, A_GROUP_WIDTH, w), F32) for w in A_WINDOWS],
        scratch_shapes=[pltpu.VMEM((tmk, A_GROUP_WIDTH), F32)],
        compiler_params=_cparams(("parallel", "arbitrary")),
        name="kv_tail_prompt",
    )(x, mod, w_in, w_in)


def _attn_p_kernel(q_ref, kp_ref, ko_ref, vp_ref, vo_ref, bias_ref, o_ref, l_ref, *, qb):
    n = pl.program_id(2)
    kall = jnp.concatenate([kp_ref[0, 0], ko_ref[0, 0]], axis=0)
    vall = jnp.concatenate([vp_ref[0, 0], vo_ref[0, 0]], axis=0)
    lane = lax.broadcasted_iota(jnp.int32, (BAND, LANES), 1)
    lo = lane < A_HEAD_DIM
    first = (n == 0).astype(jnp.int32)
    for j in range(qb // BAND):
        q = q_ref[0, 0, j * BAND:(j + 1) * BAND, :]
        kc = kall[j * BAND:(j + 2) * BAND]
        vc = vall[j * BAND:(j + 2) * BAND]
        variant = first if j == 0 else 0
        o_tiles = []
        lse_tile = jnp.zeros((BAND, LANES), F32)
        for hp in range(HEAD_PAIRS):
            cs = slice(hp * LANES, (hp + 1) * LANES)
            qp, kp, vp = q[:, cs], kc[:, cs], vc[:, cs]
            res = []
            for half in range(2):
                h = 2 * hp + half
                qm = jnp.where(lo if half == 0 else jnp.logical_not(lo), qp, jnp.zeros_like(qp))
                s = _dot_nt(qm, kp) + bias_ref[variant, h]
                m = jnp.max(s, axis=-1, keepdims=True)
                p = jnp.exp(s - m)
                l = jnp.sum(p, axis=-1, keepdims=True)
                res.append(_dot(p.astype(BF16), vp) * (1.0 / l))
                lse_tile = jnp.where(lane == h, m + jnp.log(l), lse_tile)
            o_tiles.append(jnp.where(lo, res[0], res[1]))
        o_ref[0, 0, j * BAND:(j + 1) * BAND, :] = jnp.concatenate(o_tiles, axis=1).astype(o_ref.dtype)
        l_ref[0, 0, j * BAND:(j + 1) * BAND, :] = lse_tile


def _attn_p_call(qkv_g, bias_p, g):
    bsz, dil, m, _ = qkv_g.shape
    qb = min(ATTN_QB, m)
    sub = qb // BAND
    own = lambda which: pl.BlockSpec((1, 1, qb, A_GROUP_WIDTH), lambda b, r, n: (b, r, n, which))
    prev = lambda which: pl.BlockSpec((1, 1, BAND, A_GROUP_WIDTH),
                                      lambda b, r, n: (b, r, jnp.maximum(n * sub - 1, 0), which))
    return pl.pallas_call(
        functools.partial(_attn_p_kernel, qb=qb),
        grid=(bsz, dil, m // qb),
        in_specs=[
            own(0), prev(1), own(1), prev(2), own(2),
            pl.BlockSpec((None, 2, A_HEADS, BAND, 2 * BAND), lambda b, r, n: (g, 0, 0, 0, 0)),
        ],
        out_specs=[
            pl.BlockSpec((1, 1, qb, A_GROUP_WIDTH), lambda b, r, n: (b, r, n, 0)),
            pl.BlockSpec((1, 1, qb, LANES), lambda b, r, n: (b, r, n, 0)),
        ],
        out_shape=[
            jax.ShapeDtypeStruct((bsz, dil, m, A_GROUP_WIDTH), BF16),
            jax.ShapeDtypeStruct((bsz, dil, m, LANES), F32),
        ],
        compiler_params=_cparams(("parallel", "parallel", "arbitrary")),
        name=f"attn_prompt_g{g}",
    )(qkv_g, qkv_g, qkv_g, qkv_g, qkv_g, bias_p)


def _attn_s_kernel(q_ref, new_ref, cache_ref, bias_ref, *refs, lw, aliased):
    if aliased:
        refs = refs[1:]
    out_ref, o_ref, lse_ref, pc_ref, pn_ref = refs
    ph = pl.program_id(1)
    new_t = jnp.concatenate([new_ref[0], jnp.zeros((LANES - DEC_ROWS, A_GROUP_WIDTH), F32)], axis=0).T
    lane = lax.broadcasted_iota(jnp.int32, (A_HEAD_DIM, LANES), 1)
    for h in range(A_HEADS):
        rolled = pltpu.roll(cache_ref[h], lw - DEC_ROWS, 1)
        tail = pltpu.roll(new_t[h * A_HEAD_DIM:(h + 1) * A_HEAD_DIM, :], LANES - DEC_ROWS, 1)
        if lw > LANES:
            out_ref[h, :, 0:lw - LANES] = rolled[:, 0:lw - LANES]
        out_ref[h, :, lw - LANES:lw] = jnp.where(lane >= LANES - DEC_ROWS, tail, rolled[:, lw - LANES:lw])

    lane8 = lax.broadcasted_iota(jnp.int32, (DEC_ROWS, LANES), 1)
    lo8 = lane8 < A_HEAD_DIM
    lane16 = lax.broadcasted_iota(jnp.int32, (PAD_ROWS, LANES), 1)
    lo16 = lane16 < A_HEAD_DIM
    zpad = lambda a: jnp.concatenate([a, jnp.zeros((PAD_ROWS - DEC_ROWS, a.shape[1]), F32)], axis=0)

    @pl.when(ph == 0)
    def _():
        q = zpad(q_ref[0] * QK_SCALE)
        lse_c = jnp.zeros((DEC_ROWS, LANES), F32)
        for hp in range(HEAD_PAIRS):
            qp = q[:, hp * LANES:(hp + 1) * LANES]
            kp = cache_ref[2 * hp:2 * hp + 2].reshape(LANES, lw).astype(BF16)
            kn = new_t[hp * LANES:(hp + 1) * LANES, :].astype(BF16)
            for half in range(2):
                h = 2 * hp + half
                qm = jnp.where(lo16 if half == 0 else jnp.logical_not(lo16), qp, 0.0).astype(BF16)
                s_c = _dot(qm, kp)[:DEC_ROWS] + bias_ref[h, :, 0:lw]
                s_n = _dot(qm, kn)[:DEC_ROWS] + bias_ref[h, :, lw:lw + LANES]
                m = jnp.maximum(jnp.max(s_c, axis=-1, keepdims=True), jnp.max(s_n, axis=-1, keepdims=True))
                p_c = jnp.exp(s_c - m)
                p_n = jnp.exp(s_n - m)
                l = jnp.sum(p_c, axis=-1, keepdims=True) + jnp.sum(p_n, axis=-1, keepdims=True)
                inv = 1.0 / l
                pc_ref[h] = zpad(p_c * inv)
                pn_ref[h] = zpad(p_n * inv)
                lse_c = jnp.where(lane8 == h, m + jnp.log(l), lse_c)
        lse_ref[0] = lse_c

    @pl.when(ph == 1)
    def _():
        o_tiles = []
        for hp in range(HEAD_PAIRS):
            vp = cache_ref[2 * hp:2 * hp + 2].reshape(LANES, lw).astype(BF16)
            vn = new_t[hp * LANES:(hp + 1) * LANES, :].astype(BF16)
            res = []
            for half in range(2):
                h = 2 * hp + half
                o = _dot_nt(pc_ref[h].astype(BF16), vp) + _dot_nt(pn_ref[h].astype(BF16), vn)
                res.append(o[:DEC_ROWS])
            o_tiles.append(jnp.where(lo8, res[0], res[1]))
        o_ref[0] = jnp.concatenate(o_tiles, axis=1)


def _attn_s_call(qkv_s, cache_t, prev_out, bias_s, layer, g):
    depth, bd, _, _, _, lw = cache_t.shape
    assert lw % LANES == 0
    aliased = prev_out is not None
    cache_blk = (None, None, None, A_HEADS, A_HEAD_DIM, lw)
    in_specs = [
        pl.BlockSpec((1, DEC_ROWS, A_GROUP_WIDTH), lambda b, ph: (b, 0, g)),
        pl.BlockSpec((1, DEC_ROWS, A_GROUP_WIDTH), lambda b, ph: (b, 0, A_GROUPS * (1 + ph) + g)),
        pl.BlockSpec(cache_blk, lambda b, ph: (layer, b, ph, 0, 0, 0)),
        pl.BlockSpec((A_HEADS, DEC_ROWS, lw + LANES), lambda b, ph: (0, 0, 0)),
    ]
    args = [qkv_s, qkv_s, cache_t, bias_s]
    aliases = {}
    if aliased:
        in_specs.append(pl.BlockSpec(memory_space=pl.ANY))
        args.append(prev_out)
        aliases = {4: 0}
    return pl.pallas_call(
        functools.partial(_attn_s_kernel, lw=lw, aliased=aliased),
        grid=(bd, 2),
        in_specs=in_specs,
        out_specs=[
            pl.BlockSpec(cache_blk, lambda b, ph: (layer, b, ph, 0, 0, 0)),
            pl.BlockSpec((1, DEC_ROWS, A_GROUP_WIDTH), lambda b, ph: (b, 0, 0)),
            pl.BlockSpec((1, DEC_ROWS, LANES), lambda b, ph: (b, 0, 0)),
        ],
        out_shape=[
            jax.ShapeDtypeStruct(cache_t.shape, F32),
            jax.ShapeDtypeStruct((bd, DEC_ROWS, A_GROUP_WIDTH), F32),
            jax.ShapeDtypeStruct((bd, DEC_ROWS, LANES), F32),
        ],
        scratch_shapes=[
            pltpu.VMEM((A_HEADS, PAD_ROWS, lw), F32),
            pltpu.VMEM((A_HEADS, PAD_ROWS, LANES), F32),
        ],
        input_output_aliases=aliases,
        compiler_params=_cparams(("parallel", "arbitrary")),
        name=f"attn_sample_g{g}",
    )(*args)


def _hgrn_head(q, k, f, v, st, tt):
    ti = lax.broadcasted_iota(jnp.int32, (tt, tt), 0)
    si = lax.broadcasted_iota(jnp.int32, (tt, tt), 1)
    row = lax.broadcasted_iota(jnp.int32, (tt, B_DIM), 0)
    a = jnp.where(ti == si, _dot_nt(q.astype(BF16), k.astype(BF16)), 0.0)
    qe = q * f
    ke = k
    tot = f
    lg = 0
    while (1 << lg) < tt:
        b = 1 << lg
        ab = _dot_nt(qe.astype(BF16), ke.astype(BF16))
        pair = ((ti >> lg) == (si >> lg) + 1) & ((ti >> (lg + 1)) == (si >> (lg + 1)))
        a = jnp.where(pair, ab, a)
        second = ((row >> lg) & 1) == 1
        prev_tot = pltpu.roll(tot, b, 0)
        next_tot = pltpu.roll(tot, tt - b, 0)
        qe = jnp.where(second, qe * prev_tot, qe)
        ke = jnp.where(second, ke, ke * next_tot)
        tot = tot * jnp.where(second, prev_tot, next_tot)
        lg += 1
    o = _dot(a.astype(BF16), v.astype(BF16)) + _dot_nt(qe.astype(BF16), st.astype(BF16))
    st_new = st * tot[0:1, :] + _dot(v.T.astype(BF16), ke.astype(BF16))
    return o, st_new


def _hgrn_kernel(qb_ref, fb_ref, ib_ref, gb_ref, lb_ref, nw_ref, *refs, tt, valid, has_s0):
    if has_s0:
        s0_ref, y_ref, s_out_ref, st_ref = refs
    else:
        y_ref, s_out_ref, st_ref = refs
    t = pl.program_id(1)
    nt = pl.num_programs(1)

    @pl.when(t == 0)
    def _():
        for hh in range(B_HEADS):
            if has_s0:
                st_ref[hh] = s0_ref[hh].T
            else:
                st_ref[hh] = jnp.zeros((B_DIM, B_DIM), F32)

    def load(ref):
        x = ref[0]
        if valid < tt:
            x = jnp.concatenate([x, jnp.zeros((tt - valid, x.shape[1]), F32)], axis=0)
        return x

    qb, fb, ib, gb = load(qb_ref), load(fb_ref), load(ib_ref), load(gb_ref)
    live = lax.broadcasted_iota(jnp.int32, (tt, B_DIM), 0) < valid
    for hh in range(B_HEADS):
        cs = slice(hh * B_DIM, (hh + 1) * B_DIM)
        lb = lb_ref[:, cs]
        sig = _sigmoid(fb[:, cs])
        f = jnp.maximum(lb + (1.0 - lb) * sig, TINY)
        k = (1.0 - lb) * (1.0 - sig)
        if valid < tt:
            f = jnp.where(live, f, 1.0)
            k = jnp.where(live, k, 0.0)
        o, st_new = _hgrn_head(_silu(qb[:, cs]), k, f, ib[:, cs], st_ref[hh], tt)
        st_ref[hh] = st_new
        o = o * lax.rsqrt(jnp.mean(o * o, axis=-1, keepdims=True) + RMS_EPS) * nw_ref[:, cs]
        y = o * _silu(gb[:, cs])
        y_ref[0, :, cs] = y[:valid].astype(y_ref.dtype)

    @pl.when(t == nt - 1)
    def _():
        for hh in range(B_HEADS):
            s_out_ref[0, hh] = st_ref[hh].T


def _hgrn_call(hg, lb, nw, s0, *, layer, seq_rows, name):
    bsz, s, _ = hg.shape
    tt = HGRN_TILE
    valid = min(seq_rows, tt)
    nt = s // valid
    has_s0 = s0 is not None
    blk = (1, valid, B_WIDTH)
    in_specs = [pl.BlockSpec(blk, lambda b, t, c=c: (b, t, c)) for c in range(4)]
    in_specs += [pl.BlockSpec((1, B_WIDTH), lambda b, t: (0, 0))] * 2
    args = [hg, hg, hg, hg, lb, nw]
    if has_s0:
        in_specs.append(pl.BlockSpec((None, None, B_HEADS, B_DIM, B_DIM), lambda b, t: (layer, b, 0, 0, 0)))
        args.append(s0)
    return pl.pallas_call(
        functools.partial(_hgrn_kernel, tt=tt, valid=valid, has_s0=has_s0),
        grid=(bsz, nt),
        in_specs=in_specs,
        out_specs=[
            pl.BlockSpec(blk, lambda b, t: (b, t, 0)),
            pl.BlockSpec((1, B_HEADS, B_DIM, B_DIM), lambda b, t: (b, 0, 0, 0)),
        ],
        out_shape=[
            jax.ShapeDtypeStruct((bsz, s, B_WIDTH), BF16),
            jax.ShapeDtypeStruct((bsz, B_HEADS, B_DIM, B_DIM), F32),
        ],
        scratch_shapes=[pltpu.VMEM((B_HEADS, B_DIM, B_DIM), F32)],
        compiler_params=_cparams(("parallel", "arbitrary")),
        name=name,
    )(*args)


def _token_major(ref, dil, tm, scr_ref):
    if dil == 1:
        return ref[0, 0].astype(F32)
    rows = tm // dil
    slabs = ref.shape[3] // LANES
    for r in range(dil):
        blk = ref[0, r].astype(F32)
        for lc in range(slabs):
            scr_ref[lc, pl.ds(r, rows, stride=dil), :] = blk[:, lc * LANES:(lc + 1) * LANES]
    return jnp.concatenate([scr_ref[lc] for lc in range(slabs)], axis=1)


def _merge_kernel(o0_ref, o1_ref, o2_ref, l0_ref, l1_ref, l2_ref, yb_ref, gate_ref, x_ref, mod_ref,
                  wa_ref, wb_ref, wo_ref, g_ref, b_ref, out_ref, scr_ref, *, dils, tm):
    hi = lax.broadcasted_iota(jnp.int32, (LANES, A_GROUP_WIDTH), 0)
    ci = lax.broadcasted_iota(jnp.int32, (LANES, A_GROUP_WIDTH), 1)
    spread = jnp.where((ci >> 6) == hi, 1.0, 0.0).astype(BF16)

    def widen(lse):
        p0 = lse.astype(BF16)
        r0 = lse - p0.astype(F32)
        p1 = r0.astype(BF16)
        p2 = (r0 - p1.astype(F32)).astype(BF16)
        return _dot(p0, spread) + _dot(p1, spread) + _dot(p2, spread)

    os, ls = [], []
    for o_ref, l_ref, dil in zip((o0_ref, o1_ref, o2_ref), (l0_ref, l1_ref, l2_ref), dils):
        os.append(_token_major(o_ref, dil, tm, scr_ref))
        ls.append(widen(_token_major(l_ref, dil, tm, scr_ref)))
    m = jnp.maximum(jnp.maximum(ls[0], ls[1]), ls[2])
    es = [jnp.exp(l - m) for l in ls]
    ya = (es[0] * os[0] + es[1] * os[1] + es[2] * os[2]) / (es[0] + es[1] + es[2])
    br_a = _dot(ya.astype(BF16), wa_ref[...])
    br_b = _dot(yb_ref[0].astype(BF16), wb_ref[...])
    merged = _sigmoid(gate_ref[0, :, 0:D_MODEL]) * br_a + _sigmoid(gate_ref[0, :, D_MODEL:2 * D_MODEL]) * br_b
    mix = _dot(merged.astype(BF16), wo_ref[...])
    gt = mod_ref[0, :, 2 * D_MODEL:3 * D_MODEL]
    z = ALPHA * x_ref[0] + (1.0 + gt) * mix
    out_ref[0] = _ln(z) * g_ref[...] + b_ref[...]


def _merge_call(os, lses, yb, gates, x, mod, wa, wb, wo, layer, ln_g, ln_b, *, tm, name):
    bsz, s, d = x.shape
    dils = tuple(o.shape[1] for o in os)
    tok = lambda w: pl.BlockSpec((1, tm, w), lambda b, i: (b, i, 0))
    res = lambda a: pl.BlockSpec((1, a.shape[1], tm // a.shape[1], a.shape[3]), lambda b, i: (b, 0, i, 0))
    full = lambda a: pl.BlockSpec(a.shape, lambda b, i: (0, 0))
    wsl = lambda a: pl.BlockSpec((None,) + a.shape[1:], lambda b, i: (layer, 0, 0))
    return pl.pallas_call(
        functools.partial(_merge_kernel, dils=dils, tm=tm),
        grid=(bsz, s // tm),
        in_specs=[res(a) for a in os] + [res(a) for a in lses] + [
            tok(B_WIDTH), tok(2 * D_MODEL), tok(d), _mod_spec(mod, tm, False),
            wsl(wa), wsl(wb), wsl(wo), full(ln_g), full(ln_b)],
        out_specs=tok(d),
        out_shape=jax.ShapeDtypeStruct((bsz, s, d), F32),
        scratch_shapes=[pltpu.VMEM((LANE_SLABS, tm, LANES), F32)],
        compiler_params=_cparams(("parallel", "parallel")),
        name=name,
    )(*os, *lses, yb, gates, x, mod, wa, wb, wo, ln_g, ln_b)


def _ffn_kernel(x_ref, mod_ref, wu_ref, wd_ref, g_ref, b_ref, out_ref, act_ref):
    h = _modulated(x_ref, mod_ref, 3)
    for st, sz in FF_SUBTILES:
        a = _dot(h, wu_ref[:, st:st + sz])
        gg = _dot(h, wu_ref[:, FF_DIM + st:FF_DIM + st + sz])
        act_ref[:, st:st + sz] = (_silu(a) * gg).astype(BF16)
    y = _dot(act_ref[...], wd_ref[...])
    gt = mod_ref[0, :, 5 * D_MODEL:6 * D_MODEL]
    z = ALPHA * x_ref[0] + (1.0 + gt) * y
    out_ref[0] = _ln(z) * g_ref[...] + b_ref[...]


def _ffn_call(x, mod, w_up, w_down, layer, ln_g, ln_b, *, tm, name):
    bsz, s, d = x.shape
    assert sum(sz for _, sz in FF_SUBTILES) == FF_DIM
    once = pl.Buffered(1)
    return pl.pallas_call(
        _ffn_kernel,
        grid=(bsz, s // tm),
        in_specs=[
            pl.BlockSpec((1, tm, d), lambda b, i: (b, i, 0)),
            _mod_spec(mod, tm, False),
            pl.BlockSpec((None, d, 2 * FF_DIM), lambda b, i: (layer, 0, 0), pipeline_mode=once),
            pl.BlockSpec((None, FF_DIM, d), lambda b, i: (layer, 0, 0), pipeline_mode=once),
            pl.BlockSpec((1, d), lambda b, i: (0, 0)),
            pl.BlockSpec((1, d), lambda b, i: (0, 0)),
        ],
        out_specs=pl.BlockSpec((1, tm, d), lambda b, i: (b, i, 0)),
        out_shape=jax.ShapeDtypeStruct((bsz, s, d), F32),
        scratch_shapes=[pltpu.VMEM((tm, FF_DIM), BF16)],
        compiler_params=_cparams(("parallel", "arbitrary")),
        name=name,
    )(x, mod, w_up, w_down, ln_g, ln_b)


def kernel(x_prompt, x_sample, c_prompt, c_sample, cache_kv_w128, cache_kv_w512, cache_kv_w2048, state_hgrn,
           w_in, w_branch_a, w_branch_b, w_out, hgrn_norm_w, hgrn_lb_logits, rel_bias,
           ffn_w_up, ffn_w_down, w_ada, b_ada, ln1_g, ln1_b, ln2_g, ln2_b):
    depth = w_in.shape[0]
    bp, sp, d = x_prompt.shape
    bd, sd, _ = x_sample.shape
    assert sd == DEC_ROWS and d == D_MODEL
    max_win = A_WINDOWS[-1]
    assert sp % max_win == 0
    tm_p = 1024
    tm_s = bd * sd
    n_qkv = 3 * A_QKV_WIDTH

    w_in_b = w_in.astype(BF16)
    w_rest = w_in_b[:, :, n_qkv:]
    wa_b = w_branch_a.astype(BF16)
    wb_b = w_branch_b.astype(BF16)
    wo_b = w_out.astype(BF16)
    wup_b = ffn_w_up.astype(BF16)
    wdn_b = ffn_w_down.astype(BF16)

    lbs = _lbs_call(hgrn_lb_logits)
    n_c = bp + bd
    n_c_pad = -(-n_c // 8) * 8
    c_all = jnp.concatenate([c_prompt, c_sample, jnp.zeros((n_c_pad - n_c, d), F32)], axis=0)
    mod_all = _ada_call(c_all, w_ada, b_ada)
    bias_p = _bias_p_call(rel_bias)

    caches_t = [c.transpose(0, 1, 2, 4, 5, 3) for c in (cache_kv_w128, cache_kv_w512, cache_kv_w2048)]
    bias_s = [_bias_s_call(rel_bias, g, caches_t[g].shape[-1]) for g in range(A_GROUPS)]
    new_caches = [None] * A_GROUPS
    kv_p = [[] for _ in range(A_GROUPS)]
    s_p, s_s = [], []

    xp = x_prompt
    xs = x_sample.reshape(1, tm_s, d)
    rest_ranges = ((0, 2, F32), (2, 2, F32))
    sample_ranges = ((0, 9, F32), (9, 4, F32), (13, 4, F32))
    for l in range(depth):
        mod_p = mod_all[l, 0:bp].reshape(bp, 1, N_MOD)
        mod_s = jnp.repeat(mod_all[l, bp:bp + bd], sd, axis=0).reshape(1, tm_s, N_MOD)
        lb = lbs[l:l + 1]
        nw = hgrn_norm_w[l:l + 1].astype(F32)
        g1, b1 = ln1_g[l:l + 1], ln1_b[l:l + 1]
        g2, b2 = ln2_g[l:l + 1], ln2_b[l:l + 1]

        qkv_groups = _qkv_p_call(xp, mod_p, w_in_b, l, tm=tm_p)
        hg, gates = _ln_mm_call(xp, mod_p, w_rest, l, ranges=rest_ranges, tm=tm_p, tn=1024, name="rest_proj_prompt")
        tails = _kv_tail_call(xp, mod_p, w_in_b, l)
        os, lses = [], []
        for g in range(A_GROUPS):
            o, lse = _attn_p_call(qkv_groups[g], bias_p, g)
            os.append(o)
            lses.append(lse)
            kv_p[g].append(tails[g])
        yb, st = _hgrn_call(hg, lb, nw, None, layer=l, seq_rows=HGRN_TILE, name="hgrn_prompt")
        s_p.append(st)
        xp = _merge_call(os, lses, yb, gates, xp, mod_p, wa_b, wb_b, wo_b, l, g1, b1, tm=512,
                         name="merge_prompt")
        xp = _ffn_call(xp, mod_p, wup_b, wdn_b, l, g2, b2, tm=tm_p, name="ffn_prompt")

        qkv_s, hg_s, gates_s = _ln_mm_call(xs, mod_s, w_in_b, l, ranges=sample_ranges, tm=tm_s, tn=512,
                                           name="in_proj_sample")
        qkv_s3 = qkv_s.reshape(bd, sd, n_qkv)
        os, lses = [], []
        for g in range(A_GROUPS):
            new_caches[g], o, lse = _attn_s_call(qkv_s3, caches_t[g], new_caches[g], bias_s[g], l, g)
            os.append(o.reshape(1, 1, tm_s, A_GROUP_WIDTH))
            lses.append(lse.reshape(1, 1, tm_s, LANES))
        yb_s, st_s = _hgrn_call(hg_s.reshape(bd, sd, 4 * B_WIDTH), lb, nw, state_hgrn, layer=l, seq_rows=sd,
                                name="hgrn_sample")
        s_s.append(st_s)
        xs = _merge_call(os, lses, yb_s.reshape(1, tm_s, B_WIDTH), gates_s, xs, mod_s, wa_b, wb_b, wo_b, l,
                         g1, b1, tm=tm_s, name="merge_sample")
        xs = _ffn_call(xs, mod_s, wup_b, wdn_b, l, g2, b2, tm=tm_s, name="ffn_sample")

    def rows_last(a):
        return a.transpose(0, 1, 2, 5, 3, 4)

    outs_kv_p = [rows_last(jnp.stack(kv_p[g]).reshape(depth, bp, 2, A_HEADS, A_HEAD_DIM, -1))
                 for g in range(A_GROUPS)]
    outs_kv_s = [rows_last(nc) for nc in new_caches]
    return (xp, xs.reshape(bd, sd, d),
            outs_kv_p[0], outs_kv_p[1], outs_kv_p[2], jnp.stack(s_p),
            outs_kv_s[0], outs_kv_s[1], outs_kv_s[2], jnp.stack(s_s))
```

```python
import functools
import math

import jax
import jax.numpy as jnp
import numpy as np
from jax import lax
from jax.experimental import pallas as pl
from jax.experimental.pallas import tpu as pltpu

F32 = jnp.float32
BF16 = jnp.bfloat16

D_MODEL = 1024
A_WINDOWS = (128, 512, 2048)
A_DILATIONS = (1, 4, 16)
A_GROUPS = 3
A_HEADS = 8
A_HEAD_DIM = 64
A_GROUP_WIDTH = A_HEADS * A_HEAD_DIM
A_QKV_WIDTH = A_GROUPS * A_GROUP_WIDTH
BAND = 128
NUM_BUCKETS = 32
MAX_DISTANCE = 2048
B_HEADS = 4
B_DIM = 128
B_WIDTH = B_HEADS * B_DIM
FF_DIM = 2816
N_MOD = 6 * D_MODEL
DEPTH = 4
ALPHA = (2 * DEPTH) ** 0.25
LN_EPS = 1e-5
RMS_EPS = 1e-6
NEG = -1e30
TINY = 1e-30
QK_SCALE = A_HEAD_DIM ** -0.5

LANES = 128
HEAD_PAIRS = A_GROUP_WIDTH // LANES
LANE_SLABS = A_GROUP_WIDTH // LANES
FF_SUBTILES = ((0, 512), (512, 512), (1024, 512), (1536, 512), (2048, 512), (2560, 256))
HGRN_TILE = 128
ATTN_QB = 512
DEC_ROWS = 8
PAD_ROWS = 16
VMEM_LIMIT = 56 * 1024 * 1024


def _cparams(sem):
    return pltpu.CompilerParams(dimension_semantics=sem, vmem_limit_bytes=VMEM_LIMIT)


def _sigmoid(x):
    return 1.0 / (1.0 + jnp.exp(-x))


def _silu(x):
    return x * _sigmoid(x)


def _ln(x):
    mu = jnp.mean(x, axis=-1, keepdims=True)
    xc = x - mu
    var = jnp.mean(xc * xc, axis=-1, keepdims=True)
    return xc * lax.rsqrt(var + LN_EPS)


def _dot(a, b):
    return jnp.dot(a, b, preferred_element_type=F32)


def _dot_nt(a, b):
    return lax.dot_general(a, b, (((1,), (1,)), ((), ())), preferred_element_type=F32)


def _modulated(x_ref, mod_ref, shift_chunk):
    sh = mod_ref[0, :, shift_chunk * D_MODEL:(shift_chunk + 1) * D_MODEL]
    sc = mod_ref[0, :, (shift_chunk + 1) * D_MODEL:(shift_chunk + 2) * D_MODEL]
    return (_ln(x_ref[0]) * (1.0 + sc) + sh).astype(BF16)


def _mod_spec(mod, tm, rank3):
    per_row = mod.shape[1] != 1
    if rank3:
        if per_row:
            return pl.BlockSpec((1, tm, N_MOD), lambda b, i, j: (b, i, 0))
        return pl.BlockSpec((1, 1, N_MOD), lambda b, i, j: (b, 0, 0))
    if per_row:
        return pl.BlockSpec((1, tm, N_MOD), lambda b, i: (b, i, 0))
    return pl.BlockSpec((1, 1, N_MOD), lambda b, i: (b, 0, 0))


def _lbs_kernel(lg_ref, out_ref):
    x = lg_ref[...]
    m = jnp.max(x, axis=0, keepdims=True)
    e = jnp.exp(x - m)
    p = e / jnp.sum(e, axis=0, keepdims=True)
    rows = []
    c = None
    for l in range(x.shape[0]):
        c = p[l:l + 1] if c is None else c + p[l:l + 1]
        rows.append(c - p[0:1])
    out_ref[...] = jnp.concatenate(rows, axis=0)


def _lbs_call(logits):
    return pl.pallas_call(
        _lbs_kernel, out_shape=jax.ShapeDtypeStruct(logits.shape, F32), name="hgrn_lbs",
    )(logits.astype(F32))


def _ada_kernel(c_ref, w_ref, b_ref, o_ref):
    o_ref[0] = _dot(c_ref[...].astype(BF16), w_ref[0].astype(BF16)) + b_ref[0]


def _ada_call(c_all, w_ada, b_ada):
    depth = w_ada.shape[0]
    rows = c_all.shape[0]
    tn = 512
    return pl.pallas_call(
        _ada_kernel,
        grid=(depth, N_MOD // tn),
        in_specs=[
            pl.BlockSpec((rows, D_MODEL), lambda l, j: (0, 0)),
            pl.BlockSpec((1, D_MODEL, tn), lambda l, j: (l, 0, j)),
            pl.BlockSpec((1, 1, tn), lambda l, j: (l, 0, j)),
        ],
        out_specs=pl.BlockSpec((1, rows, tn), lambda l, j: (l, 0, j)),
        out_shape=jax.ShapeDtypeStruct((depth, rows, N_MOD), F32),
        compiler_params=_cparams(("parallel", "parallel")),
        name="ada_mod",
    )(c_all, w_ada, b_ada.reshape(depth, 1, N_MOD))


def _t5_bucket_np(n):
    n = np.asarray(n, np.int32)
    max_exact = NUM_BUCKETS // 2
    nf = np.maximum(n.astype(np.float32), np.float32(max_exact))
    val = (np.log(nf / np.float32(max_exact)) / np.float32(math.log(MAX_DISTANCE / max_exact))
           * np.float32(NUM_BUCKETS - max_exact))
    large = max_exact + val.astype(np.int32)
    large = np.minimum(large, NUM_BUCKETS - 1)
    return np.where(n < max_exact, n, large).astype(np.int32)


def _bias_p_kernel(rb_ref, bk_ref, o_ref):
    g = pl.program_id(0)
    h = pl.program_id(2)
    bk = bk_ref[0, 0]
    acc = jnp.full(bk.shape, NEG, F32)
    for b in range(NUM_BUCKETS):
        acc = jnp.where(bk == b, rb_ref[b, g * A_HEADS + h], acc)
    o_ref[0, 0, 0] = acc


def _bias_p_call(rel_bias):
    qi = np.arange(BAND)[:, None]
    kj = np.arange(2 * BAND)[None, :]
    rel = qi + BAND - kj
    valid = (rel >= 0) & (rel <= BAND)
    bks = []
    for g in range(A_GROUPS):
        bk = np.where(valid, _t5_bucket_np(np.clip(rel, 0, BAND) * A_DILATIONS[g]), -1)
        bks.append(np.stack([bk, np.where(kj < BAND, -1, bk)]))
    bk_all = jnp.asarray(np.stack(bks).astype(np.int32))
    return pl.pallas_call(
        _bias_p_kernel,
        grid=(A_GROUPS, 2, A_HEADS),
        in_specs=[
            pl.BlockSpec(memory_space=pltpu.SMEM),
            pl.BlockSpec((1, 1, BAND, 2 * BAND), lambda g, f, h: (g, f, 0, 0)),
        ],
        out_specs=pl.BlockSpec((1, 1, 1, BAND, 2 * BAND), lambda g, f, h: (g, f, h, 0, 0)),
        out_shape=jax.ShapeDtypeStruct((A_GROUPS, 2, A_HEADS, BAND, 2 * BAND), F32),
        name="bias_prompt",
    )(rel_bias.astype(F32), bk_all)


def _bias_s_kernel(rb_ref, bk_ref, o_ref, *, g):
    h = pl.program_id(0)
    bk = bk_ref[...]
    acc = jnp.full(bk.shape, NEG, F32)
    for b in range(NUM_BUCKETS):
        acc = jnp.where(bk == b, rb_ref[b, g * A_HEADS + h], acc)
    o_ref[0] = acc


def _bias_s_call(rel_bias, g, lw):
    dil = A_DILATIONS[g]
    t = np.arange(DEC_ROWS)[:, None]
    col = np.arange(lw + LANES)[None, :]
    dist = lw + t - col
    ok = (dist >= 0) & (dist % dil == 0) & (dist // dil <= BAND) & (col < lw + DEC_ROWS)
    bk = np.where(ok, _t5_bucket_np(np.maximum(dist, 0)), -1).astype(np.int32)
    return pl.pallas_call(
        functools.partial(_bias_s_kernel, g=g),
        grid=(A_HEADS,),
        in_specs=[
            pl.BlockSpec(memory_space=pltpu.SMEM),
            pl.BlockSpec((DEC_ROWS, lw + LANES), lambda h: (0, 0)),
        ],
        out_specs=pl.BlockSpec((1, DEC_ROWS, lw + LANES), lambda h: (h, 0, 0)),
        out_shape=jax.ShapeDtypeStruct((A_HEADS, DEC_ROWS, lw + LANES), F32),
        name=f"bias_sample_g{g}",
    )(rel_bias.astype(F32), jnp.asarray(bk))


def _ln_mm_kernel(x_ref, mod_ref, w_ref, *refs, ranges):
    out_refs = refs[:len(ranges)]
    h_ref = refs[len(ranges)]
    j = pl.program_id(2)

    @pl.when(j == 0)
    def _():
        h_ref[...] = _modulated(x_ref, mod_ref, 0)

    for (start, cnt, _), o_ref in zip(ranges, out_refs):
        @pl.when((j >= start) & (j < start + cnt))
        def _(o_ref=o_ref):
            o_ref[0] = _dot(h_ref[...], w_ref[...]).astype(o_ref.dtype)


def _ln_mm_call(x, mod, w, layer, *, ranges, tm, tn, name):
    bsz, s, d = x.shape
    nj = sum(r[1] for r in ranges)
    assert nj * tn <= w.shape[2]
    out_specs = [
        pl.BlockSpec((1, tm, tn), lambda b, i, j, st=st, c=c: (b, i, jnp.clip(j - st, 0, c - 1)))
        for (st, c, _) in ranges
    ]
    out_shape = [jax.ShapeDtypeStruct((bsz, s, c * tn), dt) for (_, c, dt) in ranges]
    return pl.pallas_call(
        functools.partial(_ln_mm_kernel, ranges=ranges),
        grid=(bsz, s // tm, nj),
        in_specs=[
            pl.BlockSpec((1, tm, d), lambda b, i, j: (b, i, 0)),
            _mod_spec(mod, tm, True),
            pl.BlockSpec((None, d, tn), lambda b, i, j: (layer, 0, j)),
        ],
        out_specs=out_specs,
        out_shape=out_shape,
        scratch_shapes=[pltpu.VMEM((tm, d), BF16)],
        compiler_params=_cparams(("parallel", "parallel", "arbitrary")),
        name=name,
    )(x, mod, w)


def _qkv_p_kernel(x_ref, mod_ref, w_ref, o0_ref, o1_ref, o2_ref, h_ref, scr_ref, *, tm):
    j = pl.program_id(2)

    @pl.when(j == 0)
    def _():
        h_ref[...] = _modulated(x_ref, mod_ref, 0)

    h = h_ref[...]
    scale = jnp.where(j == 0, QK_SCALE, 1.0).astype(F32)
    for g, o_ref in enumerate((o0_ref, o1_ref, o2_ref)):
        dil = A_DILATIONS[g]
        acc = _dot(h, w_ref[:, g * A_GROUP_WIDTH:(g + 1) * A_GROUP_WIDTH]) * scale
        if dil == 1:
            o_ref[0, 0] = acc.astype(BF16)
            continue
        for lc in range(LANE_SLABS):
            scr_ref[lc] = acc[:, lc * LANES:(lc + 1) * LANES]
        rows = tm // dil
        for r in range(dil):
            o_ref[0, r] = jnp.concatenate(
                [scr_ref[lc, pl.ds(r, rows, stride=dil), :] for lc in range(LANE_SLABS)], axis=1).astype(BF16)


def _qkv_p_call(x, mod, w_in, layer, *, tm):
    bsz, s, d = x.shape
    out_specs, out_shape = [], []
    for g in range(A_GROUPS):
        dil = A_DILATIONS[g]
        out_specs.append(pl.BlockSpec((1, dil, tm // dil, A_GROUP_WIDTH), lambda b, i, j: (b, 0, i, j)))
        out_shape.append(jax.ShapeDtypeStruct((bsz, dil, s // dil, 3 * A_GROUP_WIDTH), BF16))
    return pl.pallas_call(
        functools.partial(_qkv_p_kernel, tm=tm),
        grid=(bsz, s // tm, 3),
        in_specs=[
            pl.BlockSpec((1, tm, d), lambda b, i, j: (b, i, 0)),
            _mod_spec(mod, tm, True),
            pl.BlockSpec((None, d, A_QKV_WIDTH), lambda b, i, j: (layer, 0, j)),
        ],
        out_specs=out_specs,
        out_shape=out_shape,
        scratch_shapes=[pltpu.VMEM((tm, d), BF16), pltpu.VMEM((LANE_SLABS, tm, LANES), F32)],
        compiler_params=_cparams(("parallel", "parallel", "arbitrary")),
        name="qkv_prompt",
    )(x, mod, w_in)


def _kv_tail_kernel(x_ref, mod_ref, wk_ref, wv_ref, o0_ref, o1_ref, o2_ref, acc_ref, *, tmk):
    i = pl.program_id(1)
    last = pl.num_programs(1) - 1
    h = _modulated(x_ref, mod_ref, 0)

    def kv_t(rows, kv, g):
        w_ref = wv_ref if kv else wk_ref
        n = rows.shape[0]
        acc_ref[0:n, :] = _dot(rows, w_ref[:, g * A_GROUP_WIDTH:(g + 1) * A_GROUP_WIDTH])
        return acc_ref[0:n, :].T

    for kv in range(2):
        o2_ref[0, kv] = kv_t(h, kv, 2)

    @pl.when(i == last)
    def _():
        for kv in range(2):
            o1_ref[0, kv] = kv_t(h, kv, 1)
            o0_ref[0, kv] = kv_t(h[tmk - A_WINDOWS[0]:, :], kv, 0)


def _kv_tail_call(x, mod, w_in, layer):
    bsz, s, d = x.shape
    tmk = A_WINDOWS[1]
    span = A_WINDOWS[2]
    nt = span // tmk
    off = (s - span) // tmk
    return pl.pallas_call(
        functools.partial(_kv_tail_kernel, tmk=tmk),
        grid=(bsz, nt),
        in_specs=[
            pl.BlockSpec((1, tmk, d), lambda b, i: (b, i + off, 0)),
            _mod_spec(mod, tmk, False),
            pl.BlockSpec((None, d, A_QKV_WIDTH), lambda b, i: (layer, 0, 1)),
            pl.BlockSpec((None, d, A_QKV_WIDTH), lambda b, i: (layer, 0, 2)),
        ],
        out_specs=[
            pl.BlockSpec((1, 2, A_GROUP_WIDTH, A_WINDOWS[0]), lambda b, i: (b, 0, 0, 0)),
            pl.BlockSpec((1, 2, A_GROUP_WIDTH, A_WINDOWS[1]), lambda b, i: (b, 0, 0, 0)),
            pl.BlockSpec((1, 2, A_GROUP_WIDTH, tmk), lambda b, i: (b, 0, 0, i)),
        ],
        out_shape=[jax.ShapeDtypeStruct((bsz, 2, A_GROUP_WIDTH, w), F32) for w in A_WINDOWS],
        scratch_shapes=[pltpu.VMEM((tmk, A_GROUP_WIDTH), F32)],
        compiler_params=_cparams(("parallel", "arbitrary")),
        name="kv_tail_prompt",
    )(x, mod, w_in, w_in)


def _attn_p_kernel(q_ref, kp_ref, ko_ref, vp_ref, vo_ref, bias_ref, o_ref, l_ref, *, qb):
    n = pl.program_id(2)
    kall = jnp.concatenate([kp_ref[0, 0], ko_ref[0, 0]], axis=0)
    vall = jnp.concatenate([vp_ref[0, 0], vo_ref[0, 0]], axis=0)
    lane = lax.broadcasted_iota(jnp.int32, (BAND, LANES), 1)
    lo = lane < A_HEAD_DIM
    first = (n == 0).astype(jnp.int32)
    for j in range(qb // BAND):
        q = q_ref[0, 0, j * BAND:(j + 1) * BAND, :]
        kc = kall[j * BAND:(j + 2) * BAND]
        vc = vall[j * BAND:(j + 2) * BAND]
        variant = first if j == 0 else 0
        o_tiles = []
        lse_tile = jnp.zeros((BAND, LANES), F32)
        for hp in range(HEAD_PAIRS):
            cs = slice(hp * LANES, (hp + 1) * LANES)
            qp, kp, vp = q[:, cs], kc[:, cs], vc[:, cs]
            res = []
            for half in range(2):
                h = 2 * hp + half
                qm = jnp.where(lo if half == 0 else jnp.logical_not(lo), qp, jnp.zeros_like(qp))
                s = _dot_nt(qm, kp) + bias_ref[variant, h]
                m = jnp.max(s, axis=-1, keepdims=True)
                p = jnp.exp(s - m)
                l = jnp.sum(p, axis=-1, keepdims=True)
                res.append(_dot(p.astype(BF16), vp) * (1.0 / l))
                lse_tile = jnp.where(lane == h, m + jnp.log(l), lse_tile)
            o_tiles.append(jnp.where(lo, res[0], res[1]))
        o_ref[0, 0, j * BAND:(j + 1) * BAND, :] = jnp.concatenate(o_tiles, axis=1).astype(o_ref.dtype)
        l_ref[0, 0, j * BAND:(j + 1) * BAND, :] = lse_tile


def _attn_p_call(qkv_g, bias_p, g):
    bsz, dil, m, _ = qkv_g.shape
    qb = min(ATTN_QB, m)
    sub = qb // BAND
    own = lambda which: pl.BlockSpec((1, 1, qb, A_GROUP_WIDTH), lambda b, r, n: (b, r, n, which))
    prev = lambda which: pl.BlockSpec((1, 1, BAND, A_GROUP_WIDTH),
                                      lambda b, r, n: (b, r, jnp.maximum(n * sub - 1, 0), which))
    return pl.pallas_call(
        functools.partial(_attn_p_kernel, qb=qb),
        grid=(bsz, dil, m // qb),
        in_specs=[
            own(0), prev(1), own(1), prev(2), own(2),
            pl.BlockSpec((None, 2, A_HEADS, BAND, 2 * BAND), lambda b, r, n: (g, 0, 0, 0, 0)),
        ],
        out_specs=[
            pl.BlockSpec((1, 1, qb, A_GROUP_WIDTH), lambda b, r, n: (b, r, n, 0)),
            pl.BlockSpec((1, 1, qb, LANES), lambda b, r, n: (b, r, n, 0)),
        ],
        out_shape=[
            jax.ShapeDtypeStruct((bsz, dil, m, A_GROUP_WIDTH), BF16),
            jax.ShapeDtypeStruct((bsz, dil, m, LANES), F32),
        ],
        compiler_params=_cparams(("parallel", "parallel", "arbitrary")),
        name=f"attn_prompt_g{g}",
    )(qkv_g, qkv_g, qkv_g, qkv_g, qkv_g, bias_p)


def _attn_s_kernel(q_ref, kn_ref, vn_ref, cache_ref, bias_ref, *refs, lw, nb, kvb, aliased):
    if aliased:
        refs = refs[1:]
    out_ref, o_ref, lse_ref, pc_ref, pn_ref = refs
    ph = pl.program_id(1)
    lane = lax.broadcasted_iota(jnp.int32, (A_HEAD_DIM, LANES), 1)
    lane8 = lax.broadcasted_iota(jnp.int32, (DEC_ROWS, LANES), 1)
    lo8 = lane8 < A_HEAD_DIM
    lane16 = lax.broadcasted_iota(jnp.int32, (PAD_ROWS, LANES), 1)
    lo16 = lane16 < A_HEAD_DIM
    zpad = lambda a: jnp.concatenate([a, jnp.zeros((PAD_ROWS - DEC_ROWS, a.shape[1]), F32)], axis=0)

    def transposed(new):
        return jnp.concatenate([new, jnp.zeros((LANES - DEC_ROWS, A_GROUP_WIDTH), F32)], axis=0).T

    def roll_write(bi, kvi, new_t):
        for h in range(A_HEADS):
            rolled = pltpu.roll(cache_ref[bi, kvi, h], lw - DEC_ROWS, 1)
            tail = pltpu.roll(new_t[h * A_HEAD_DIM:(h + 1) * A_HEAD_DIM, :], LANES - DEC_ROWS, 1)
            if lw > LANES:
                out_ref[bi, kvi, h, :, 0:lw - LANES] = rolled[:, 0:lw - LANES]
            out_ref[bi, kvi, h, :, lw - LANES:lw] = jnp.where(lane >= LANES - DEC_ROWS, tail,
                                                              rolled[:, lw - LANES:lw])

    def scores(bi, kvi, new_t):
        q = zpad(q_ref[bi] * QK_SCALE)
        lse_c = jnp.zeros((DEC_ROWS, LANES), F32)
        for hp in range(HEAD_PAIRS):
            qp = q[:, hp * LANES:(hp + 1) * LANES]
            kp = cache_ref[bi, kvi, 2 * hp:2 * hp + 2].reshape(LANES, lw).astype(BF16)
            kn = new_t[hp * LANES:(hp + 1) * LANES, :].astype(BF16)
            for half in range(2):
                h = 2 * hp + half
                qm = jnp.where(lo16 if half == 0 else jnp.logical_not(lo16), qp, 0.0).astype(BF16)
                s_c = _dot(qm, kp)[:DEC_ROWS] + bias_ref[h, :, 0:lw]
                s_n = _dot(qm, kn)[:DEC_ROWS] + bias_ref[h, :, lw:lw + LANES]
                m = jnp.maximum(jnp.max(s_c, axis=-1, keepdims=True), jnp.max(s_n, axis=-1, keepdims=True))
                p_c = jnp.exp(s_c - m)
                p_n = jnp.exp(s_n - m)
                l = jnp.sum(p_c, axis=-1, keepdims=True) + jnp.sum(p_n, axis=-1, keepdims=True)
                inv = 1.0 / l
                pc_ref[h] = zpad(p_c * inv)
                pn_ref[h] = zpad(p_n * inv)
                lse_c = jnp.where(lane8 == h, m + jnp.log(l), lse_c)
        lse_ref[bi] = lse_c

    def values(bi, kvi, new_t):
        o_tiles = []
        for hp in range(HEAD_PAIRS):
            vp = cache_ref[bi, kvi, 2 * hp:2 * hp + 2].reshape(LANES, lw).astype(BF16)
            vn = new_t[hp * LANES:(hp + 1) * LANES, :].astype(BF16)
            res = []
            for half in range(2):
                h = 2 * hp + half
                o = _dot_nt(pc_ref[h].astype(BF16), vp) + _dot_nt(pn_ref[h].astype(BF16), vn)
                res.append(o[:DEC_ROWS])
            o_tiles.append(jnp.where(lo8, res[0], res[1]))
        o_ref[bi] = jnp.concatenate(o_tiles, axis=1)

    def one_sequence(bi):
        if kvb == 2:
            k_t = transposed(kn_ref[bi])
            roll_write(bi, 0, k_t)
            scores(bi, 0, k_t)
            v_t = transposed(vn_ref[bi])
            roll_write(bi, 1, v_t)
            values(bi, 1, v_t)
        else:
            new_t = transposed(jnp.where(ph == 0, kn_ref[bi], vn_ref[bi]))
            roll_write(bi, 0, new_t)
            pl.when(ph == 0)(lambda: scores(bi, 0, new_t))
            pl.when(ph == 1)(lambda: values(bi, 0, new_t))

    if nb == 1:
        one_sequence(0)
    else:
        def body(bi, carry):
            one_sequence(bi)
            return carry
        lax.fori_loop(0, nb, body, 0)


SAMPLE_STEP = {128: (8, 2), 512: (2, 2), 2048: (1, 1)}


def _attn_s_call(qkv_s, cache_t, prev_out, bias_s, layer, g):
    depth, bd, _, _, _, lw = cache_t.shape
    assert lw % LANES == 0
    nb, kvb = SAMPLE_STEP.get(lw, (1, 1))
    nb = math.gcd(nb, bd)
    aliased = prev_out is not None
    cache_blk = (None, nb, kvb, A_HEADS, A_HEAD_DIM, lw)
    cache_idx = (lambda b, ph: (layer, b, ph, 0, 0, 0)) if kvb == 1 else (lambda b, ph: (layer, b, 0, 0, 0, 0))
    new_blk = (nb, DEC_ROWS, A_GROUP_WIDTH)
    in_specs = [
        pl.BlockSpec(new_blk, lambda b, ph: (b, 0, g)),
        pl.BlockSpec(new_blk, lambda b, ph: (b, 0, A_GROUPS + g)),
        pl.BlockSpec(new_blk, lambda b, ph: (b, 0, 2 * A_GROUPS + g)),
        pl.BlockSpec(cache_blk, cache_idx),
        pl.BlockSpec((A_HEADS, DEC_ROWS, lw + LANES), lambda b, ph: (0, 0, 0)),
    ]
    args = [qkv_s, qkv_s, qkv_s, cache_t, bias_s]
    aliases = {}
    if aliased:
        in_specs.append(pl.BlockSpec(memory_space=pl.ANY))
        args.append(prev_out)
        aliases = {5: 0}
    return pl.pallas_call(
        functools.partial(_attn_s_kernel, lw=lw, nb=nb, kvb=kvb, aliased=aliased),
        grid=(bd // nb, 2 // kvb),
        in_specs=in_specs,
        out_specs=[
            pl.BlockSpec(cache_blk, cache_idx),
            pl.BlockSpec(new_blk, lambda b, ph: (b, 0, 0)),
            pl.BlockSpec((nb, DEC_ROWS, LANES), lambda b, ph: (b, 0, 0)),
        ],
        out_shape=[
            jax.ShapeDtypeStruct(cache_t.shape, F32),
            jax.ShapeDtypeStruct((bd, DEC_ROWS, A_GROUP_WIDTH), F32),
            jax.ShapeDtypeStruct((bd, DEC_ROWS, LANES), F32),
        ],
        scratch_shapes=[
            pltpu.VMEM((A_HEADS, PAD_ROWS, lw), F32),
            pltpu.VMEM((A_HEADS, PAD_ROWS, LANES), F32),
        ],
        input_output_aliases=aliases,
        compiler_params=_cparams(("parallel", "arbitrary")),
        name=f"attn_sample_g{g}",
    )(*args)


def _hgrn_head(q, k, f, v, st, tt):
    ti = lax.broadcasted_iota(jnp.int32, (tt, tt), 0)
    si = lax.broadcasted_iota(jnp.int32, (tt, tt), 1)
    row = lax.broadcasted_iota(jnp.int32, (tt, B_DIM), 0)
    a = jnp.where(ti == si, _dot_nt(q.astype(BF16), k.astype(BF16)), 0.0)
    qe = q * f
    ke = k
    tot = f
    lg = 0
    while (1 << lg) < tt:
        b = 1 << lg
        ab = _dot_nt(qe.astype(BF16), ke.astype(BF16))
        pair = ((ti >> lg) == (si >> lg) + 1) & ((ti >> (lg + 1)) == (si >> (lg + 1)))
        a = jnp.where(pair, ab, a)
        second = ((row >> lg) & 1) == 1
        prev_tot = pltpu.roll(tot, b, 0)
        next_tot = pltpu.roll(tot, tt - b, 0)
        qe = jnp.where(second, qe * prev_tot, qe)
        ke = jnp.where(second, ke, ke * next_tot)
        tot = tot * jnp.where(second, prev_tot, next_tot)
        lg += 1
    o = _dot(a.astype(BF16), v.astype(BF16)) + _dot_nt(qe.astype(BF16), st.astype(BF16))
    st_new = st * tot[0:1, :] + _dot(v.T.astype(BF16), ke.astype(BF16))
    return o, st_new


def _hgrn_kernel(qb_ref, fb_ref, ib_ref, gb_ref, lb_ref, nw_ref, *refs, tt, valid, has_s0):
    if has_s0:
        s0_ref, y_ref, s_out_ref, st_ref = refs
    else:
        y_ref, s_out_ref, st_ref = refs
    t = pl.program_id(1)
    nt = pl.num_programs(1)

    @pl.when(t == 0)
    def _():
        for hh in range(B_HEADS):
            if has_s0:
                st_ref[hh] = s0_ref[hh].T
            else:
                st_ref[hh] = jnp.zeros((B_DIM, B_DIM), F32)

    def load(ref):
        x = ref[0]
        if valid < tt:
            x = jnp.concatenate([x, jnp.zeros((tt - valid, x.shape[1]), F32)], axis=0)
        return x

    qb, fb, ib, gb = load(qb_ref), load(fb_ref), load(ib_ref), load(gb_ref)
    live = lax.broadcasted_iota(jnp.int32, (tt, B_DIM), 0) < valid
    for hh in range(B_HEADS):
        cs = slice(hh * B_DIM, (hh + 1) * B_DIM)
        lb = lb_ref[:, cs]
        sig = _sigmoid(fb[:, cs])
        f = jnp.maximum(lb + (1.0 - lb) * sig, TINY)
        k = (1.0 - lb) * (1.0 - sig)
        if valid < tt:
            f = jnp.where(live, f, 1.0)
            k = jnp.where(live, k, 0.0)
        o, st_new = _hgrn_head(_silu(qb[:, cs]), k, f, ib[:, cs], st_ref[hh], tt)
        st_ref[hh] = st_new
        o = o * lax.rsqrt(jnp.mean(o * o, axis=-1, keepdims=True) + RMS_EPS) * nw_ref[:, cs]
        y = o * _silu(gb[:, cs])
        y_ref[0, :, cs] = y[:valid].astype(y_ref.dtype)

    @pl.when(t == nt - 1)
    def _():
        for hh in range(B_HEADS):
            s_out_ref[0, hh] = st_ref[hh].T


def _hgrn_call(hg, lb, nw, s0, *, layer, seq_rows, name):
    bsz, s, _ = hg.shape
    tt = HGRN_TILE
    valid = min(seq_rows, tt)
    nt = s // valid
    has_s0 = s0 is not None
    blk = (1, valid, B_WIDTH)
    in_specs = [pl.BlockSpec(blk, lambda b, t, c=c: (b, t, c)) for c in range(4)]
    in_specs += [pl.BlockSpec((1, B_WIDTH), lambda b, t: (0, 0))] * 2
    args = [hg, hg, hg, hg, lb, nw]
    if has_s0:
        in_specs.append(pl.BlockSpec((None, None, B_HEADS, B_DIM, B_DIM), lambda b, t: (layer, b, 0, 0, 0)))
        args.append(s0)
    return pl.pallas_call(
        functools.partial(_hgrn_kernel, tt=tt, valid=valid, has_s0=has_s0),
        grid=(bsz, nt),
        in_specs=in_specs,
        out_specs=[
            pl.BlockSpec(blk, lambda b, t: (b, t, 0)),
            pl.BlockSpec((1, B_HEADS, B_DIM, B_DIM), lambda b, t: (b, 0, 0, 0)),
        ],
        out_shape=[
            jax.ShapeDtypeStruct((bsz, s, B_WIDTH), BF16),
            jax.ShapeDtypeStruct((bsz, B_HEADS, B_DIM, B_DIM), F32),
        ],
        scratch_shapes=[pltpu.VMEM((B_HEADS, B_DIM, B_DIM), F32)],
        compiler_params=_cparams(("parallel", "arbitrary")),
        name=name,
    )(*args)


def _token_major(ref, dil, tm, scr_ref):
    if dil == 1:
        return ref[0, 0].astype(F32)
    rows = tm // dil
    slabs = ref.shape[3] // LANES
    for r in range(dil):
        blk = ref[0, r].astype(F32)
        for lc in range(slabs):
            scr_ref[lc, pl.ds(r, rows, stride=dil), :] = blk[:, lc * LANES:(lc + 1) * LANES]
    return jnp.concatenate([scr_ref[lc] for lc in range(slabs)], axis=1)


def _merge_kernel(o0_ref, o1_ref, o2_ref, l0_ref, l1_ref, l2_ref, yb_ref, x_ref, mod_ref,
                  wg_ref, wa_ref, wb_ref, wo_ref, g_ref, b_ref, out_ref, scr_ref, *, dils, tm):
    hi = lax.broadcasted_iota(jnp.int32, (LANES, A_GROUP_WIDTH), 0)
    ci = lax.broadcasted_iota(jnp.int32, (LANES, A_GROUP_WIDTH), 1)
    spread = jnp.where((ci >> 6) == hi, 1.0, 0.0).astype(BF16)

    def widen(lse):
        p0 = lse.astype(BF16)
        r0 = lse - p0.astype(F32)
        p1 = r0.astype(BF16)
        p2 = (r0 - p1.astype(F32)).astype(BF16)
        return _dot(p0, spread) + _dot(p1, spread) + _dot(p2, spread)

    os, ls = [], []
    for o_ref, l_ref, dil in zip((o0_ref, o1_ref, o2_ref), (l0_ref, l1_ref, l2_ref), dils):
        os.append(_token_major(o_ref, dil, tm, scr_ref))
        ls.append(widen(_token_major(l_ref, dil, tm, scr_ref)))
    m = jnp.maximum(jnp.maximum(ls[0], ls[1]), ls[2])
    es = [jnp.exp(l - m) for l in ls]
    ya = (es[0] * os[0] + es[1] * os[1] + es[2] * os[2]) / (es[0] + es[1] + es[2])
    br_a = _dot(ya.astype(BF16), wa_ref[...])
    br_b = _dot(yb_ref[0].astype(BF16), wb_ref[...])
    h = _modulated(x_ref, mod_ref, 0)
    gate_a = _dot(h, wg_ref[:, 0:D_MODEL])
    gate_b = _dot(h, wg_ref[:, D_MODEL:2 * D_MODEL])
    merged = _sigmoid(gate_a) * br_a + _sigmoid(gate_b) * br_b
    mix = _dot(merged.astype(BF16), wo_ref[...])
    gt = mod_ref[0, :, 2 * D_MODEL:3 * D_MODEL]
    z = ALPHA * x_ref[0] + (1.0 + gt) * mix
    out_ref[0] = _ln(z) * g_ref[...] + b_ref[...]


def _merge_call(os, lses, yb, x, mod, wg, wa, wb, wo, layer, ln_g, ln_b, *, tm, name):
    bsz, s, d = x.shape
    dils = tuple(o.shape[1] for o in os)
    tok = lambda w: pl.BlockSpec((1, tm, w), lambda b, i: (b, i, 0))
    res = lambda a: pl.BlockSpec((1, a.shape[1], tm // a.shape[1], a.shape[3]), lambda b, i: (b, 0, i, 0))
    full = lambda a: pl.BlockSpec(a.shape, lambda b, i: (0, 0))
    wsl = lambda a: pl.BlockSpec((None,) + a.shape[1:], lambda b, i: (layer, 0, 0), pipeline_mode=pl.Buffered(1))
    return pl.pallas_call(
        functools.partial(_merge_kernel, dils=dils, tm=tm),
        grid=(bsz, s // tm),
        in_specs=[res(a) for a in os] + [res(a) for a in lses] + [
            tok(B_WIDTH), tok(d), _mod_spec(mod, tm, False),
            wsl(wg), wsl(wa), wsl(wb), wsl(wo), full(ln_g), full(ln_b)],
        out_specs=tok(d),
        out_shape=jax.ShapeDtypeStruct((bsz, s, d), F32),
        scratch_shapes=[pltpu.VMEM((LANE_SLABS, tm, LANES), F32)],
        compiler_params=_cparams(("parallel", "parallel")),
        name=name,
    )(*os, *lses, yb, x, mod, wg, wa, wb, wo, ln_g, ln_b)


def _ffn_kernel(x_ref, mod_ref, wu_ref, wd_ref, g_ref, b_ref, out_ref, act_ref):
    h = _modulated(x_ref, mod_ref, 3)
    for st, sz in FF_SUBTILES:
        a = _dot(h, wu_ref[:, st:st + sz])
        gg = _dot(h, wu_ref[:, FF_DIM + st:FF_DIM + st + sz])
        act_ref[:, st:st + sz] = (_silu(a) * gg).astype(BF16)
    y = _dot(act_ref[...], wd_ref[...])
    gt = mod_ref[0, :, 5 * D_MODEL:6 * D_MODEL]
    z = ALPHA * x_ref[0] + (1.0 + gt) * y
    out_ref[0] = _ln(z) * g_ref[...] + b_ref[...]


def _ffn_call(x, mod, w_up, w_down, layer, ln_g, ln_b, *, tm, name):
    bsz, s, d = x.shape
    assert sum(sz for _, sz in FF_SUBTILES) == FF_DIM
    once = pl.Buffered(1)
    return pl.pallas_call(
        _ffn_kernel,
        grid=(bsz, s // tm),
        in_specs=[
            pl.BlockSpec((1, tm, d), lambda b, i: (b, i, 0)),
            _mod_spec(mod, tm, False),
            pl.BlockSpec((None, d, 2 * FF_DIM), lambda b, i: (layer, 0, 0), pipeline_mode=once),
            pl.BlockSpec((None, FF_DIM, d), lambda b, i: (layer, 0, 0), pipeline_mode=once),
            pl.BlockSpec((1, d), lambda b, i: (0, 0)),
            pl.BlockSpec((1, d), lambda b, i: (0, 0)),
        ],
        out_specs=pl.BlockSpec((1, tm, d), lambda b, i: (b, i, 0)),
        out_shape=jax.ShapeDtypeStruct((bsz, s, d), F32),
        scratch_shapes=[pltpu.VMEM((tm, FF_DIM), BF16)],
        compiler_params=_cparams(("parallel", "arbitrary")),
        name=name,
    )(x, mod, w_up, w_down, ln_g, ln_b)


def kernel(x_prompt, x_sample, c_prompt, c_sample, cache_kv_w128, cache_kv_w512, cache_kv_w2048, state_hgrn,
           w_in, w_branch_a, w_branch_b, w_out, hgrn_norm_w, hgrn_lb_logits, rel_bias,
           ffn_w_up, ffn_w_down, w_ada, b_ada, ln1_g, ln1_b, ln2_g, ln2_b):
    depth = w_in.shape[0]
    bp, sp, d = x_prompt.shape
    bd, sd, _ = x_sample.shape
    assert sd == DEC_ROWS and d == D_MODEL
    max_win = A_WINDOWS[-1]
    assert sp % max_win == 0
    tm_p = 1024
    tm_s = bd * sd
    n_qkv = 3 * A_QKV_WIDTH

    w_in_b = w_in.astype(BF16)
    w_hg = w_in_b[:, :, n_qkv:n_qkv + 4 * B_WIDTH]
    w_gate = w_in_b[:, :, n_qkv + 4 * B_WIDTH:]
    wa_b = w_branch_a.astype(BF16)
    wb_b = w_branch_b.astype(BF16)
    wo_b = w_out.astype(BF16)
    wup_b = ffn_w_up.astype(BF16)
    wdn_b = ffn_w_down.astype(BF16)

    lbs = _lbs_call(hgrn_lb_logits)
    n_c = bp + bd
    n_c_pad = -(-n_c // 8) * 8
    c_all = jnp.concatenate([c_prompt, c_sample, jnp.zeros((n_c_pad - n_c, d), F32)], axis=0)
    mod_all = _ada_call(c_all, w_ada, b_ada)
    bias_p = _bias_p_call(rel_bias)

    caches_t = [c.transpose(0, 1, 2, 4, 5, 3) for c in (cache_kv_w128, cache_kv_w512, cache_kv_w2048)]
    bias_s = [_bias_s_call(rel_bias, g, caches_t[g].shape[-1]) for g in range(A_GROUPS)]
    new_caches = [None] * A_GROUPS
    kv_p = [[] for _ in range(A_GROUPS)]
    s_p, s_s = [], []

    xp = x_prompt
    xs = x_sample.reshape(1, tm_s, d)
    hg_ranges = ((0, 2, F32),)
    sample_ranges = ((0, 9, F32), (9, 4, F32))
    for l in range(depth):
        mod_p = mod_all[l, 0:bp].reshape(bp, 1, N_MOD)
        mod_s = jnp.repeat(mod_all[l, bp:bp + bd], sd, axis=0).reshape(1, tm_s, N_MOD)
        lb = lbs[l:l + 1]
        nw = hgrn_norm_w[l:l + 1].astype(F32)
        g1, b1 = ln1_g[l:l + 1], ln1_b[l:l + 1]
        g2, b2 = ln2_g[l:l + 1], ln2_b[l:l + 1]

        qkv_groups = _qkv_p_call(xp, mod_p, w_in_b, l, tm=tm_p)
        (hg,) = _ln_mm_call(xp, mod_p, w_hg, l, ranges=hg_ranges, tm=tm_p, tn=1024, name="hg_proj_prompt")
        tails = _kv_tail_call(xp, mod_p, w_in_b, l)
        os, lses = [], []
        for g in range(A_GROUPS):
            o, lse = _attn_p_call(qkv_groups[g], bias_p, g)
            os.append(o)
            lses.append(lse)
            kv_p[g].append(tails[g])
        yb, st = _hgrn_call(hg, lb, nw, None, layer=l, seq_rows=HGRN_TILE, name="hgrn_prompt")
        s_p.append(st)
        xp = _merge_call(os, lses, yb, xp, mod_p, w_gate, wa_b, wb_b, wo_b, l, g1, b1, tm=512,
                         name="merge_prompt")
        xp = _ffn_call(xp, mod_p, wup_b, wdn_b, l, g2, b2, tm=tm_p, name="ffn_prompt")

        qkv_s, hg_s = _ln_mm_call(xs, mod_s, w_in_b, l, ranges=sample_ranges, tm=tm_s, tn=512,
                                  name="in_proj_sample")
        qkv_s3 = qkv_s.reshape(bd, sd, n_qkv)
        os, lses = [], []
        for g in range(A_GROUPS):
            new_caches[g], o, lse = _attn_s_call(qkv_s3, caches_t[g], new_caches[g], bias_s[g], l, g)
            os.append(o.reshape(1, 1, tm_s, A_GROUP_WIDTH))
            lses.append(lse.reshape(1, 1, tm_s, LANES))
        yb_s, st_s = _hgrn_call(hg_s.reshape(bd, sd, 4 * B_WIDTH), lb, nw, state_hgrn, layer=l, seq_rows=sd,
                                name="hgrn_sample")
        s_s.append(st_s)
        xs = _merge_call(os, lses, yb_s.reshape(1, tm_s, B_WIDTH), xs, mod_s, w_gate, wa_b, wb_b, wo_b, l,
                         g1, b1, tm=tm_s, name="merge_sample")
        xs = _ffn_call(xs, mod_s, wup_b, wdn_b, l, g2, b2, tm=tm_s, name="ffn_sample")

    def rows_last(a):
        return a.transpose(0, 1, 2, 5, 3, 4)

    outs_kv_p = [rows_last(jnp.stack(kv_p[g]).reshape(depth, bp, 2, A_HEADS, A_HEAD_DIM, -1))
                 for g in range(A_GROUPS)]
    outs_kv_s = [rows_last(nc) for nc in new_caches]
    return (xp, xs.reshape(bd, sd, d),
            outs_kv_p[0], outs_kv_p[1], outs_kv_p[2], jnp.stack(s_p),
            outs_kv_s[0], outs_kv_s[1], outs_kv_s[2], jnp.stack(s_s))
```

```python
import functools
import math

import jax
import jax.numpy as jnp
import numpy as np
from jax import lax
from jax.experimental import pallas as pl
from jax.experimental.pallas import tpu as pltpu

F32 = jnp.float32
BF16 = jnp.bfloat16

D_MODEL = 1024
A_WINDOWS = (128, 512, 2048)
A_DILATIONS = (1, 4, 16)
A_GROUPS = 3
A_HEADS = 8
A_HEAD_DIM = 64
A_GROUP_WIDTH = A_HEADS * A_HEAD_DIM
A_QKV_WIDTH = A_GROUPS * A_GROUP_WIDTH
BAND = 128
NUM_BUCKETS = 32
MAX_DISTANCE = 2048
B_HEADS = 4
B_DIM = 128
B_WIDTH = B_HEADS * B_DIM
FF_DIM = 2816
N_MOD = 6 * D_MODEL
DEPTH = 4
ALPHA = (2 * DEPTH) ** 0.25
LN_EPS = 1e-5
RMS_EPS = 1e-6
NEG = -1e30
TINY = 1e-30
QK_SCALE = A_HEAD_DIM ** -0.5

LANES = 128
HEAD_PAIRS = A_GROUP_WIDTH // LANES
LANE_SLABS = A_GROUP_WIDTH // LANES
FF_SUBTILES = ((0, 512), (512, 512), (1024, 512), (1536, 512), (2048, 512), (2560, 256))
HGRN_TILE = 128
ATTN_QB = 512
DEC_ROWS = 8
PAD_ROWS = 16
VMEM_LIMIT = 56 * 1024 * 1024


def _cparams(sem):
    return pltpu.CompilerParams(dimension_semantics=sem, vmem_limit_bytes=VMEM_LIMIT)


def _sigmoid(x):
    return 1.0 / (1.0 + jnp.exp(-x))


def _silu(x):
    return x * _sigmoid(x)


def _ln(x):
    mu = jnp.mean(x, axis=-1, keepdims=True)
    xc = x - mu
    var = jnp.mean(xc * xc, axis=-1, keepdims=True)
    return xc * lax.rsqrt(var + LN_EPS)


def _dot(a, b):
    return jnp.dot(a, b, preferred_element_type=F32)


def _dot_nt(a, b):
    return lax.dot_general(a, b, (((1,), (1,)), ((), ())), preferred_element_type=F32)


def _modulated(x_ref, mod_ref, shift_chunk):
    sh = mod_ref[0, :, shift_chunk * D_MODEL:(shift_chunk + 1) * D_MODEL]
    sc = mod_ref[0, :, (shift_chunk + 1) * D_MODEL:(shift_chunk + 2) * D_MODEL]
    return (_ln(x_ref[0]) * (1.0 + sc) + sh).astype(BF16)


def _mod_spec(mod, tm, rank3):
    per_row = mod.shape[1] != 1
    if rank3:
        if per_row:
            return pl.BlockSpec((1, tm, N_MOD), lambda b, i, j: (b, i, 0))
        return pl.BlockSpec((1, 1, N_MOD), lambda b, i, j: (b, 0, 0))
    if per_row:
        return pl.BlockSpec((1, tm, N_MOD), lambda b, i: (b, i, 0))
    return pl.BlockSpec((1, 1, N_MOD), lambda b, i: (b, 0, 0))


def _lbs_kernel(lg_ref, out_ref):
    x = lg_ref[...]
    m = jnp.max(x, axis=0, keepdims=True)
    e = jnp.exp(x - m)
    p = e / jnp.sum(e, axis=0, keepdims=True)
    rows = []
    c = None
    for l in range(x.shape[0]):
        c = p[l:l + 1] if c is None else c + p[l:l + 1]
        rows.append(c - p[0:1])
    out_ref[...] = jnp.concatenate(rows, axis=0)


def _lbs_call(logits):
    return pl.pallas_call(
        _lbs_kernel, out_shape=jax.ShapeDtypeStruct(logits.shape, F32), name="hgrn_lbs",
    )(logits.astype(F32))


def _ada_kernel(c_ref, w_ref, b_ref, o_ref):
    o_ref[0] = _dot(c_ref[...].astype(BF16), w_ref[0].astype(BF16)) + b_ref[0]


def _ada_call(c_all, w_ada, b_ada):
    depth = w_ada.shape[0]
    rows = c_all.shape[0]
    tn = 512
    return pl.pallas_call(
        _ada_kernel,
        grid=(depth, N_MOD // tn),
        in_specs=[
            pl.BlockSpec((rows, D_MODEL), lambda l, j: (0, 0)),
            pl.BlockSpec((1, D_MODEL, tn), lambda l, j: (l, 0, j)),
            pl.BlockSpec((1, 1, tn), lambda l, j: (l, 0, j)),
        ],
        out_specs=pl.BlockSpec((1, rows, tn), lambda l, j: (l, 0, j)),
        out_shape=jax.ShapeDtypeStruct((depth, rows, N_MOD), F32),
        compiler_params=_cparams(("parallel", "parallel")),
        name="ada_mod",
    )(c_all, w_ada, b_ada.reshape(depth, 1, N_MOD))


def _t5_bucket_np(n):
    n = np.asarray(n, np.int32)
    max_exact = NUM_BUCKETS // 2
    nf = np.maximum(n.astype(np.float32), np.float32(max_exact))
    val = (np.log(nf / np.float32(max_exact)) / np.float32(math.log(MAX_DISTANCE / max_exact))
           * np.float32(NUM_BUCKETS - max_exact))
    large = max_exact + val.astype(np.int32)
    large = np.minimum(large, NUM_BUCKETS - 1)
    return np.where(n < max_exact, n, large).astype(np.int32)


def _bias_p_kernel(rb_ref, bk_ref, o_ref):
    g = pl.program_id(0)
    h = pl.program_id(2)
    bk = bk_ref[0, 0]
    acc = jnp.full(bk.shape, NEG, F32)
    for b in range(NUM_BUCKETS):
        acc = jnp.where(bk == b, rb_ref[b, g * A_HEADS + h], acc)
    o_ref[0, 0, 0] = acc


def _bias_p_call(rel_bias):
    qi = np.arange(BAND)[:, None]
    kj = np.arange(2 * BAND)[None, :]
    rel = qi + BAND - kj
    valid = (rel >= 0) & (rel <= BAND)
    bks = []
    for g in range(A_GROUPS):
        bk = np.where(valid, _t5_bucket_np(np.clip(rel, 0, BAND) * A_DILATIONS[g]), -1)
        bks.append(np.stack([bk, np.where(kj < BAND, -1, bk)]))
    bk_all = jnp.asarray(np.stack(bks).astype(np.int32))
    return pl.pallas_call(
        _bias_p_kernel,
        grid=(A_GROUPS, 2, A_HEADS),
        in_specs=[
            pl.BlockSpec(memory_space=pltpu.SMEM),
            pl.BlockSpec((1, 1, BAND, 2 * BAND), lambda g, f, h: (g, f, 0, 0)),
        ],
        out_specs=pl.BlockSpec((1, 1, 1, BAND, 2 * BAND), lambda g, f, h: (g, f, h, 0, 0)),
        out_shape=jax.ShapeDtypeStruct((A_GROUPS, 2, A_HEADS, BAND, 2 * BAND), F32),
        name="bias_prompt",
    )(rel_bias.astype(F32), bk_all)


def _bias_s_kernel(rb_ref, bk_ref, o_ref, *, g):
    h = pl.program_id(0)
    bk = bk_ref[...]
    acc = jnp.full(bk.shape, NEG, F32)
    for b in range(NUM_BUCKETS):
        acc = jnp.where(bk == b, rb_ref[b, g * A_HEADS + h], acc)
    o_ref[0] = acc


def _bias_s_call(rel_bias, g, lw):
    dil = A_DILATIONS[g]
    t = np.arange(DEC_ROWS)[:, None]
    col = np.arange(lw + LANES)[None, :]
    dist = lw + t - col
    ok = (dist >= 0) & (dist % dil == 0) & (dist // dil <= BAND) & (col < lw + DEC_ROWS)
    bk = np.where(ok, _t5_bucket_np(np.maximum(dist, 0)), -1).astype(np.int32)
    return pl.pallas_call(
        functools.partial(_bias_s_kernel, g=g),
        grid=(A_HEADS,),
        in_specs=[
            pl.BlockSpec(memory_space=pltpu.SMEM),
            pl.BlockSpec((DEC_ROWS, lw + LANES), lambda h: (0, 0)),
        ],
        out_specs=pl.BlockSpec((1, DEC_ROWS, lw + LANES), lambda h: (h, 0, 0)),
        out_shape=jax.ShapeDtypeStruct((A_HEADS, DEC_ROWS, lw + LANES), F32),
        name=f"bias_sample_g{g}",
    )(rel_bias.astype(F32), jnp.asarray(bk))


def _ln_mm_kernel(x_ref, mod_ref, w_ref, *refs, ranges):
    out_refs = refs[:len(ranges)]
    h_ref = refs[len(ranges)]
    j = pl.program_id(2)

    @pl.when(j == 0)
    def _():
        h_ref[...] = _modulated(x_ref, mod_ref, 0)

    for (start, cnt, _), o_ref in zip(ranges, out_refs):
        @pl.when((j >= start) & (j < start + cnt))
        def _(o_ref=o_ref):
            o_ref[0] = _dot(h_ref[...], w_ref[...]).astype(o_ref.dtype)


def _ln_mm_call(x, mod, w, layer, *, ranges, tm, tn, name):
    bsz, s, d = x.shape
    nj = sum(r[1] for r in ranges)
    assert nj * tn <= w.shape[2]
    out_specs = [
        pl.BlockSpec((1, tm, tn), lambda b, i, j, st=st, c=c: (b, i, jnp.clip(j - st, 0, c - 1)))
        for (st, c, _) in ranges
    ]
    out_shape = [jax.ShapeDtypeStruct((bsz, s, c * tn), dt) for (_, c, dt) in ranges]
    return pl.pallas_call(
        functools.partial(_ln_mm_kernel, ranges=ranges),
        grid=(bsz, s // tm, nj),
        in_specs=[
            pl.BlockSpec((1, tm, d), lambda b, i, j: (b, i, 0)),
            _mod_spec(mod, tm, True),
            pl.BlockSpec((None, d, tn), lambda b, i, j: (layer, 0, j)),
        ],
        out_specs=out_specs,
        out_shape=out_shape,
        scratch_shapes=[pltpu.VMEM((tm, d), BF16)],
        compiler_params=_cparams(("parallel", "parallel", "arbitrary")),
        name=name,
    )(x, mod, w)


def _qkv_p_kernel(x_ref, mod_ref, w_ref, o0_ref, o1_ref, o2_ref, scr_ref, *, tm):
    h = _modulated(x_ref, mod_ref, 0)
    for which in range(3):
        for g, o_ref in enumerate((o0_ref, o1_ref, o2_ref)):
            dil = A_DILATIONS[g]
            c0 = (which * A_GROUPS + g) * A_GROUP_WIDTH
            acc = _dot(h, w_ref[:, c0:c0 + A_GROUP_WIDTH])
            if which == 0:
                acc = acc * QK_SCALE
            cols = slice(which * A_GROUP_WIDTH, (which + 1) * A_GROUP_WIDTH)
            if dil == 1:
                o_ref[0, 0, :, cols] = acc.astype(BF16)
                continue
            slab = (which * A_GROUPS + g) % 2
            for lc in range(LANE_SLABS):
                scr_ref[slab, lc] = acc[:, lc * LANES:(lc + 1) * LANES]
            rows = tm // dil
            for r in range(dil):
                o_ref[0, r, :, cols] = jnp.concatenate(
                    [scr_ref[slab, lc, pl.ds(r, rows, stride=dil), :] for lc in range(LANE_SLABS)],
                    axis=1).astype(BF16)


def _qkv_p_call(x, mod, w_in, layer, *, tm):
    bsz, s, d = x.shape
    out_specs, out_shape = [], []
    for g in range(A_GROUPS):
        dil = A_DILATIONS[g]
        out_specs.append(pl.BlockSpec((1, dil, tm // dil, 3 * A_GROUP_WIDTH), lambda b, i: (b, 0, i, 0)))
        out_shape.append(jax.ShapeDtypeStruct((bsz, dil, s // dil, 3 * A_GROUP_WIDTH), BF16))
    return pl.pallas_call(
        functools.partial(_qkv_p_kernel, tm=tm),
        grid=(bsz, s // tm),
        in_specs=[
            pl.BlockSpec((1, tm, d), lambda b, i: (b, i, 0)),
            _mod_spec(mod, tm, False),
            pl.BlockSpec((None, d, 3 * A_QKV_WIDTH), lambda b, i: (layer, 0, 0), pipeline_mode=pl.Buffered(1)),
        ],
        out_specs=out_specs,
        out_shape=out_shape,
        scratch_shapes=[pltpu.VMEM((2, LANE_SLABS, tm, LANES), F32)],
        compiler_params=_cparams(("parallel", "arbitrary")),
        name="qkv_prompt",
    )(x, mod, w_in)


def _kv_tail_kernel(x_ref, mod_ref, wk_ref, wv_ref, o0_ref, o1_ref, o2_ref, acc_ref, *, tmk):
    i = pl.program_id(1)
    last = pl.num_programs(1) - 1
    h = _modulated(x_ref, mod_ref, 0)

    def kv_t(rows, kv, g):
        w_ref = wv_ref if kv else wk_ref
        n = rows.shape[0]
        acc_ref[0:n, :] = _dot(rows, w_ref[:, g * A_GROUP_WIDTH:(g + 1) * A_GROUP_WIDTH])
        return acc_ref[0:n, :].T

    for kv in range(2):
        o2_ref[0, kv] = kv_t(h, kv, 2)

    @pl.when(i == last)
    def _():
        for kv in range(2):
            o1_ref[0, kv] = kv_t(h, kv, 1)
            o0_ref[0, kv] = kv_t(h[tmk - A_WINDOWS[0]:, :], kv, 0)


def _kv_tail_call(x, mod, w_in, layer):
    bsz, s, d = x.shape
    tmk = A_WINDOWS[1]
    span = A_WINDOWS[2]
    nt = span // tmk
    off = (s - span) // tmk
    return pl.pallas_call(
        functools.partial(_kv_tail_kernel, tmk=tmk),
        grid=(bsz, nt),
        in_specs=[
            pl.BlockSpec((1, tmk, d), lambda b, i: (b, i + off, 0)),
            _mod_spec(mod, tmk, False),
            pl.BlockSpec((None, d, A_QKV_WIDTH), lambda b, i: (layer, 0, 1)),
            pl.BlockSpec((None, d, A_QKV_WIDTH), lambda b, i: (layer, 0, 2)),
        ],
        out_specs=[
            pl.BlockSpec((1, 2, A_GROUP_WIDTH, A_WINDOWS[0]), lambda b, i: (b, 0, 0, 0)),
            pl.BlockSpec((1, 2, A_GROUP_WIDTH, A_WINDOWS[1]), lambda b, i: (b, 0, 0, 0)),
            pl.BlockSpec((1, 2, A_GROUP_WIDTH, tmk), lambda b, i: (b, 0, 0, i)),
        ],
        out_shape=[jax.ShapeDtypeStruct((bsz, 2, A_GROUP_WIDTH, w), F32) for w in A_WINDOWS],
        scratch_shapes=[pltpu.VMEM((tmk, A_GROUP_WIDTH), F32)],
        compiler_params=_cparams(("parallel", "arbitrary")),
        name="kv_tail_prompt",
    )(x, mod, w_in, w_in)


def _attn_p_kernel(q_ref, kp_ref, ko_ref, vp_ref, vo_ref, bias_ref, o_ref, l_ref, *, qb):
    n = pl.program_id(2)
    kall = jnp.concatenate([kp_ref[0, 0], ko_ref[0, 0]], axis=0)
    vall = jnp.concatenate([vp_ref[0, 0], vo_ref[0, 0]], axis=0)
    lane = lax.broadcasted_iota(jnp.int32, (BAND, LANES), 1)
    lo = lane < A_HEAD_DIM
    first = (n == 0).astype(jnp.int32)
    for j in range(qb // BAND):
        q = q_ref[0, 0, j * BAND:(j + 1) * BAND, :]
        kc = kall[j * BAND:(j + 2) * BAND]
        vc = vall[j * BAND:(j + 2) * BAND]
        variant = first if j == 0 else 0
        o_tiles = []
        lse_tile = jnp.zeros((BAND, LANES), F32)
        for hp in range(HEAD_PAIRS):
            cs = slice(hp * LANES, (hp + 1) * LANES)
            qp, kp, vp = q[:, cs], kc[:, cs], vc[:, cs]
            res = []
            for half in range(2):
                h = 2 * hp + half
                qm = jnp.where(lo if half == 0 else jnp.logical_not(lo), qp, jnp.zeros_like(qp))
                s = _dot_nt(qm, kp) + bias_ref[variant, h]
                m = jnp.max(s, axis=-1, keepdims=True)
                p = jnp.exp(s - m)
                l = jnp.sum(p, axis=-1, keepdims=True)
                res.append(_dot(p.astype(BF16), vp) * (1.0 / l))
                lse_tile = jnp.where(lane == h, m + jnp.log(l), lse_tile)
            o_tiles.append(jnp.where(lo, res[0], res[1]))
        o_ref[0, 0, j * BAND:(j + 1) * BAND, :] = jnp.concatenate(o_tiles, axis=1).astype(o_ref.dtype)
        l_ref[0, 0, j * BAND:(j + 1) * BAND, :] = lse_tile


def _attn_p_call(qkv_g, bias_p, g):
    bsz, dil, m, _ = qkv_g.shape
    qb = min(ATTN_QB, m)
    sub = qb // BAND
    own = lambda which: pl.BlockSpec((1, 1, qb, A_GROUP_WIDTH), lambda b, r, n: (b, r, n, which))
    prev = lambda which: pl.BlockSpec((1, 1, BAND, A_GROUP_WIDTH),
                                      lambda b, r, n: (b, r, jnp.maximum(n * sub - 1, 0), which))
    return pl.pallas_call(
        functools.partial(_attn_p_kernel, qb=qb),
        grid=(bsz, dil, m // qb),
        in_specs=[
            own(0), prev(1), own(1), prev(2), own(2),
            pl.BlockSpec((None, 2, A_HEADS, BAND, 2 * BAND), lambda b, r, n: (g, 0, 0, 0, 0)),
        ],
        out_specs=[
            pl.BlockSpec((1, 1, qb, A_GROUP_WIDTH), lambda b, r, n: (b, r, n, 0)),
            pl.BlockSpec((1, 1, qb, LANES), lambda b, r, n: (b, r, n, 0)),
        ],
        out_shape=[
            jax.ShapeDtypeStruct((bsz, dil, m, A_GROUP_WIDTH), BF16),
            jax.ShapeDtypeStruct((bsz, dil, m, LANES), F32),
        ],
        compiler_params=_cparams(("parallel", "parallel", "arbitrary")),
        name=f"attn_prompt_g{g}",
    )(qkv_g, qkv_g, qkv_g, qkv_g, qkv_g, bias_p)


def _attn_s_kernel(q_ref, kn_ref, vn_ref, cache_ref, bias_ref, *refs, lw, nb, kvb, aliased):
    if aliased:
        refs = refs[1:]
    out_ref, o_ref, lse_ref, pc_ref, pn_ref = refs
    ph = pl.program_id(1)
    lane = lax.broadcasted_iota(jnp.int32, (A_HEAD_DIM, LANES), 1)
    lane8 = lax.broadcasted_iota(jnp.int32, (DEC_ROWS, LANES), 1)
    lo8 = lane8 < A_HEAD_DIM
    lane16 = lax.broadcasted_iota(jnp.int32, (PAD_ROWS, LANES), 1)
    lo16 = lane16 < A_HEAD_DIM
    zpad = lambda a: jnp.concatenate([a, jnp.zeros((PAD_ROWS - DEC_ROWS, a.shape[1]), F32)], axis=0)

    def transposed(new):
        return jnp.concatenate([new, jnp.zeros((LANES - DEC_ROWS, A_GROUP_WIDTH), F32)], axis=0).T

    def roll_write(bi, kvi, new_t):
        for h in range(A_HEADS):
            rolled = pltpu.roll(cache_ref[bi, kvi, h], lw - DEC_ROWS, 1)
            tail = pltpu.roll(new_t[h * A_HEAD_DIM:(h + 1) * A_HEAD_DIM, :], LANES - DEC_ROWS, 1)
            if lw > LANES:
                out_ref[bi, kvi, h, :, 0:lw - LANES] = rolled[:, 0:lw - LANES]
            out_ref[bi, kvi, h, :, lw - LANES:lw] = jnp.where(lane >= LANES - DEC_ROWS, tail,
                                                              rolled[:, lw - LANES:lw])

    def scores(bi, kvi, new_t):
        q = zpad(q_ref[bi] * QK_SCALE)
        lse_c = jnp.zeros((DEC_ROWS, LANES), F32)
        for hp in range(HEAD_PAIRS):
            qp = q[:, hp * LANES:(hp + 1) * LANES]
            kp = cache_ref[bi, kvi, 2 * hp:2 * hp + 2].reshape(LANES, lw).astype(BF16)
            kn = new_t[hp * LANES:(hp + 1) * LANES, :].astype(BF16)
            for half in range(2):
                h = 2 * hp + half
                qm = jnp.where(lo16 if half == 0 else jnp.logical_not(lo16), qp, 0.0).astype(BF16)
                s_c = _dot(qm, kp)[:DEC_ROWS] + bias_ref[h, :, 0:lw]
                s_n = _dot(qm, kn)[:DEC_ROWS] + bias_ref[h, :, lw:lw + LANES]
                m = jnp.maximum(jnp.max(s_c, axis=-1, keepdims=True), jnp.max(s_n, axis=-1, keepdims=True))
                p_c = jnp.exp(s_c - m)
                p_n = jnp.exp(s_n - m)
                l = jnp.sum(p_c, axis=-1, keepdims=True) + jnp.sum(p_n, axis=-1, keepdims=True)
                inv = 1.0 / l
                pc_ref[h] = zpad(p_c * inv)
                pn_ref[h] = zpad(p_n * inv)
                lse_c = jnp.where(lane8 == h, m + jnp.log(l), lse_c)
        lse_ref[bi] = lse_c

    def values(bi, kvi, new_t):
        o_tiles = []
        for hp in range(HEAD_PAIRS):
            vp = cache_ref[bi, kvi, 2 * hp:2 * hp + 2].reshape(LANES, lw).astype(BF16)
            vn = new_t[hp * LANES:(hp + 1) * LANES, :].astype(BF16)
            res = []
            for half in range(2):
                h = 2 * hp + half
                o = _dot_nt(pc_ref[h].astype(BF16), vp) + _dot_nt(pn_ref[h].astype(BF16), vn)
                res.append(o[:DEC_ROWS])
            o_tiles.append(jnp.where(lo8, res[0], res[1]))
        o_ref[bi] = jnp.concatenate(o_tiles, axis=1)

    def one_sequence(bi):
        if kvb == 2:
            k_t = transposed(kn_ref[bi])
            roll_write(bi, 0, k_t)
            scores(bi, 0, k_t)
            v_t = transposed(vn_ref[bi])
            roll_write(bi, 1, v_t)
            values(bi, 1, v_t)
        else:
            new_t = transposed(jnp.where(ph == 0, kn_ref[bi], vn_ref[bi]))
            roll_write(bi, 0, new_t)
            pl.when(ph == 0)(lambda: scores(bi, 0, new_t))
            pl.when(ph == 1)(lambda: values(bi, 0, new_t))

    if nb == 1:
        one_sequence(0)
    else:
        def body(bi, carry):
            one_sequence(bi)
            return carry
        lax.fori_loop(0, nb, body, 0)


SAMPLE_STEP = {128: (8, 2), 512: (2, 2), 2048: (1, 1)}


def _attn_s_call(qkv_s, cache_t, prev_out, bias_s, layer, g):
    depth, bd, _, _, _, lw = cache_t.shape
    assert lw % LANES == 0
    nb, kvb = SAMPLE_STEP.get(lw, (1, 1))
    nb = math.gcd(nb, bd)
    aliased = prev_out is not None
    cache_blk = (None, nb, kvb, A_HEADS, A_HEAD_DIM, lw)
    cache_idx = (lambda b, ph: (layer, b, ph, 0, 0, 0)) if kvb == 1 else (lambda b, ph: (layer, b, 0, 0, 0, 0))
    new_blk = (nb, DEC_ROWS, A_GROUP_WIDTH)
    in_specs = [
        pl.BlockSpec(new_blk, lambda b, ph: (b, 0, g)),
        pl.BlockSpec(new_blk, lambda b, ph: (b, 0, A_GROUPS + g)),
        pl.BlockSpec(new_blk, lambda b, ph: (b, 0, 2 * A_GROUPS + g)),
        pl.BlockSpec(cache_blk, cache_idx),
        pl.BlockSpec((A_HEADS, DEC_ROWS, lw + LANES), lambda b, ph: (0, 0, 0)),
    ]
    args = [qkv_s, qkv_s, qkv_s, cache_t, bias_s]
    aliases = {}
    if aliased:
        in_specs.append(pl.BlockSpec(memory_space=pl.ANY))
        args.append(prev_out)
        aliases = {5: 0}
    return pl.pallas_call(
        functools.partial(_attn_s_kernel, lw=lw, nb=nb, kvb=kvb, aliased=aliased),
        grid=(bd // nb, 2 // kvb),
        in_specs=in_specs,
        out_specs=[
            pl.BlockSpec(cache_blk, cache_idx),
            pl.BlockSpec(new_blk, lambda b, ph: (b, 0, 0)),
            pl.BlockSpec((nb, DEC_ROWS, LANES), lambda b, ph: (b, 0, 0)),
        ],
        out_shape=[
            jax.ShapeDtypeStruct(cache_t.shape, F32),
            jax.ShapeDtypeStruct((bd, DEC_ROWS, A_GROUP_WIDTH), F32),
            jax.ShapeDtypeStruct((bd, DEC_ROWS, LANES), F32),
        ],
        scratch_shapes=[
            pltpu.VMEM((A_HEADS, PAD_ROWS, lw), F32),
            pltpu.VMEM((A_HEADS, PAD_ROWS, LANES), F32),
        ],
        input_output_aliases=aliases,
        compiler_params=_cparams(("parallel", "arbitrary")),
        name=f"attn_sample_g{g}",
    )(*args)


def _hgrn_head(q, k, f, v, st, tt):
    ti = lax.broadcasted_iota(jnp.int32, (tt, tt), 0)
    si = lax.broadcasted_iota(jnp.int32, (tt, tt), 1)
    row = lax.broadcasted_iota(jnp.int32, (tt, B_DIM), 0)
    a = jnp.where(ti == si, _dot_nt(q.astype(BF16), k.astype(BF16)), 0.0)
    qe = q * f
    ke = k
    tot = f
    lg = 0
    while (1 << lg) < tt:
        b = 1 << lg
        ab = _dot_nt(qe.astype(BF16), ke.astype(BF16))
        pair = ((ti >> lg) == (si >> lg) + 1) & ((ti >> (lg + 1)) == (si >> (lg + 1)))
        a = jnp.where(pair, ab, a)
        second = ((row >> lg) & 1) == 1
        prev_tot = pltpu.roll(tot, b, 0)
        next_tot = pltpu.roll(tot, tt - b, 0)
        qe = jnp.where(second, qe * prev_tot, qe)
        ke = jnp.where(second, ke, ke * next_tot)
        tot = tot * jnp.where(second, prev_tot, next_tot)
        lg += 1
    o = _dot(a.astype(BF16), v.astype(BF16)) + _dot_nt(qe.astype(BF16), st.astype(BF16))
    st_new = st * tot[0:1, :] + _dot(v.T.astype(BF16), ke.astype(BF16))
    return o, st_new


def _hgrn_kernel(qb_ref, fb_ref, ib_ref, gb_ref, lb_ref, nw_ref, *refs, tt, valid, has_s0):
    if has_s0:
        s0_ref, y_ref, s_out_ref, st_ref = refs
    else:
        y_ref, s_out_ref, st_ref = refs
    t = pl.program_id(1)
    nt = pl.num_programs(1)

    @pl.when(t == 0)
    def _():
        for hh in range(B_HEADS):
            if has_s0:
                st_ref[hh] = s0_ref[hh].T
            else:
                st_ref[hh] = jnp.zeros((B_DIM, B_DIM), F32)

    def load(ref):
        x = ref[0]
        if valid < tt:
            x = jnp.concatenate([x, jnp.zeros((tt - valid, x.shape[1]), F32)], axis=0)
        return x

    qb, fb, ib, gb = load(qb_ref), load(fb_ref), load(ib_ref), load(gb_ref)
    live = lax.broadcasted_iota(jnp.int32, (tt, B_DIM), 0) < valid
    for hh in range(B_HEADS):
        cs = slice(hh * B_DIM, (hh + 1) * B_DIM)
        lb = lb_ref[:, cs]
        sig = _sigmoid(fb[:, cs])
        f = jnp.maximum(lb + (1.0 - lb) * sig, TINY)
        k = (1.0 - lb) * (1.0 - sig)
        if valid < tt:
            f = jnp.where(live, f, 1.0)
            k = jnp.where(live, k, 0.0)
        o, st_new = _hgrn_head(_silu(qb[:, cs]), k, f, ib[:, cs], st_ref[hh], tt)
        st_ref[hh] = st_new
        o = o * lax.rsqrt(jnp.mean(o * o, axis=-1, keepdims=True) + RMS_EPS) * nw_ref[:, cs]
        y = o * _silu(gb[:, cs])
        y_ref[0, :, cs] = y[:valid].astype(y_ref.dtype)

    @pl.when(t == nt - 1)
    def _():
        for hh in range(B_HEADS):
            s_out_ref[0, hh] = st_ref[hh].T


def _hgrn_call(hg, lb, nw, s0, *, layer, seq_rows, name):
    bsz, s, _ = hg.shape
    tt = HGRN_TILE
    valid = min(seq_rows, tt)
    nt = s // valid
    has_s0 = s0 is not None
    blk = (1, valid, B_WIDTH)
    in_specs = [pl.BlockSpec(blk, lambda b, t, c=c: (b, t, c)) for c in range(4)]
    in_specs += [pl.BlockSpec((1, B_WIDTH), lambda b, t: (0, 0))] * 2
    args = [hg, hg, hg, hg, lb, nw]
    if has_s0:
        in_specs.append(pl.BlockSpec((None, None, B_HEADS, B_DIM, B_DIM), lambda b, t: (layer, b, 0, 0, 0)))
        args.append(s0)
    return pl.pallas_call(
        functools.partial(_hgrn_kernel, tt=tt, valid=valid, has_s0=has_s0),
        grid=(bsz, nt),
        in_specs=in_specs,
        out_specs=[
            pl.BlockSpec(blk, lambda b, t: (b, t, 0)),
            pl.BlockSpec((1, B_HEADS, B_DIM, B_DIM), lambda b, t: (b, 0, 0, 0)),
        ],
        out_shape=[
            jax.ShapeDtypeStruct((bsz, s, B_WIDTH), BF16),
            jax.ShapeDtypeStruct((bsz, B_HEADS, B_DIM, B_DIM), F32),
        ],
        scratch_shapes=[pltpu.VMEM((B_HEADS, B_DIM, B_DIM), F32)],
        compiler_params=_cparams(("parallel", "arbitrary")),
        name=name,
    )(*args)


def _token_major(ref, dil, tm, scr_ref):
    if dil == 1:
        return ref[0, 0].astype(F32)
    rows = tm // dil
    slabs = ref.shape[3] // LANES
    for r in range(dil):
        blk = ref[0, r].astype(F32)
        for lc in range(slabs):
            scr_ref[lc, pl.ds(r, rows, stride=dil), :] = blk[:, lc * LANES:(lc + 1) * LANES]
    return jnp.concatenate([scr_ref[lc] for lc in range(slabs)], axis=1)


def _merge_kernel(o0_ref, o1_ref, o2_ref, l0_ref, l1_ref, l2_ref, yb_ref, x_ref, mod_ref,
                  wg_ref, wa_ref, wb_ref, wo_ref, g_ref, b_ref, out_ref, scr_ref, *, dils, tm):
    hi = lax.broadcasted_iota(jnp.int32, (LANES, A_GROUP_WIDTH), 0)
    ci = lax.broadcasted_iota(jnp.int32, (LANES, A_GROUP_WIDTH), 1)
    spread = jnp.where((ci >> 6) == hi, 1.0, 0.0).astype(BF16)

    def widen(lse):
        p0 = lse.astype(BF16)
        r0 = lse - p0.astype(F32)
        p1 = r0.astype(BF16)
        p2 = (r0 - p1.astype(F32)).astype(BF16)
        return _dot(p0, spread) + _dot(p1, spread) + _dot(p2, spread)

    os, ls = [], []
    for o_ref, l_ref, dil in zip((o0_ref, o1_ref, o2_ref), (l0_ref, l1_ref, l2_ref), dils):
        os.append(_token_major(o_ref, dil, tm, scr_ref))
        ls.append(widen(_token_major(l_ref, dil, tm, scr_ref)))
    m = jnp.maximum(jnp.maximum(ls[0], ls[1]), ls[2])
    es = [jnp.exp(l - m) for l in ls]
    ya = (es[0] * os[0] + es[1] * os[1] + es[2] * os[2]) / (es[0] + es[1] + es[2])
    br_a = _dot(ya.astype(BF16), wa_ref[...])
    br_b = _dot(yb_ref[0].astype(BF16), wb_ref[...])
    h = _modulated(x_ref, mod_ref, 0)
    gate_a = _dot(h, wg_ref[:, 0:D_MODEL])
    gate_b = _dot(h, wg_ref[:, D_MODEL:2 * D_MODEL])
    merged = _sigmoid(gate_a) * br_a + _sigmoid(gate_b) * br_b
    mix = _dot(merged.astype(BF16), wo_ref[...])
    gt = mod_ref[0, :, 2 * D_MODEL:3 * D_MODEL]
    z = ALPHA * x_ref[0] + (1.0 + gt) * mix
    out_ref[0] = _ln(z) * g_ref[...] + b_ref[...]


def _merge_call(os, lses, yb, x, mod, wg, wa, wb, wo, layer, ln_g, ln_b, *, tm, name):
    bsz, s, d = x.shape
    dils = tuple(o.shape[1] for o in os)
    tok = lambda w: pl.BlockSpec((1, tm, w), lambda b, i: (b, i, 0))
    res = lambda a: pl.BlockSpec((1, a.shape[1], tm // a.shape[1], a.shape[3]), lambda b, i: (b, 0, i, 0))
    full = lambda a: pl.BlockSpec(a.shape, lambda b, i: (0, 0))
    wsl = lambda a: pl.BlockSpec((None,) + a.shape[1:], lambda b, i: (layer, 0, 0), pipeline_mode=pl.Buffered(1))
    return pl.pallas_call(
        functools.partial(_merge_kernel, dils=dils, tm=tm),
        grid=(bsz, s // tm),
        in_specs=[res(a) for a in os] + [res(a) for a in lses] + [
            tok(B_WIDTH), tok(d), _mod_spec(mod, tm, False),
            wsl(wg), wsl(wa), wsl(wb), wsl(wo), full(ln_g), full(ln_b)],
        out_specs=tok(d),
        out_shape=jax.ShapeDtypeStruct((bsz, s, d), F32),
        scratch_shapes=[pltpu.VMEM((LANE_SLABS, tm, LANES), F32)],
        compiler_params=_cparams(("parallel", "parallel")),
        name=name,
    )(*os, *lses, yb, x, mod, wg, wa, wb, wo, ln_g, ln_b)


def _ffn_kernel(x_ref, mod_ref, wu_ref, wd_ref, g_ref, b_ref, out_ref, act_ref):
    h = _modulated(x_ref, mod_ref, 3)
    for st, sz in FF_SUBTILES:
        a = _dot(h, wu_ref[:, st:st + sz])
        gg = _dot(h, wu_ref[:, FF_DIM + st:FF_DIM + st + sz])
        act_ref[:, st:st + sz] = (_silu(a) * gg).astype(BF16)
    y = _dot(act_ref[...], wd_ref[...])
    gt = mod_ref[0, :, 5 * D_MODEL:6 * D_MODEL]
    z = ALPHA * x_ref[0] + (1.0 + gt) * y
    out_ref[0] = _ln(z) * g_ref[...] + b_ref[...]


def _ffn_call(x, mod, w_up, w_down, layer, ln_g, ln_b, *, tm, name):
    bsz, s, d = x.shape
    assert sum(sz for _, sz in FF_SUBTILES) == FF_DIM
    once = pl.Buffered(1)
    return pl.pallas_call(
        _ffn_kernel,
        grid=(bsz, s // tm),
        in_specs=[
            pl.BlockSpec((1, tm, d), lambda b, i: (b, i, 0)),
            _mod_spec(mod, tm, False),
            pl.BlockSpec((None, d, 2 * FF_DIM), lambda b, i: (layer, 0, 0), pipeline_mode=once),
            pl.BlockSpec((None, FF_DIM, d), lambda b, i: (layer, 0, 0), pipeline_mode=once),
            pl.BlockSpec((1, d), lambda b, i: (0, 0)),
            pl.BlockSpec((1, d), lambda b, i: (0, 0)),
        ],
        out_specs=pl.BlockSpec((1, tm, d), lambda b, i: (b, i, 0)),
        out_shape=jax.ShapeDtypeStruct((bsz, s, d), F32),
        scratch_shapes=[pltpu.VMEM((tm, FF_DIM), BF16)],
        compiler_params=_cparams(("parallel", "arbitrary")),
        name=name,
    )(x, mod, w_up, w_down, ln_g, ln_b)


def kernel(x_prompt, x_sample, c_prompt, c_sample, cache_kv_w128, cache_kv_w512, cache_kv_w2048, state_hgrn,
           w_in, w_branch_a, w_branch_b, w_out, hgrn_norm_w, hgrn_lb_logits, rel_bias,
           ffn_w_up, ffn_w_down, w_ada, b_ada, ln1_g, ln1_b, ln2_g, ln2_b):
    depth = w_in.shape[0]
    bp, sp, d = x_prompt.shape
    bd, sd, _ = x_sample.shape
    assert sd == DEC_ROWS and d == D_MODEL
    max_win = A_WINDOWS[-1]
    assert sp % max_win == 0
    tm_p = 1024
    tm_s = bd * sd
    n_qkv = 3 * A_QKV_WIDTH

    w_in_b = w_in.astype(BF16)
    w_hg = w_in_b[:, :, n_qkv:n_qkv + 4 * B_WIDTH]
    w_gate = w_in_b[:, :, n_qkv + 4 * B_WIDTH:]
    wa_b = w_branch_a.astype(BF16)
    wb_b = w_branch_b.astype(BF16)
    wo_b = w_out.astype(BF16)
    wup_b = ffn_w_up.astype(BF16)
    wdn_b = ffn_w_down.astype(BF16)

    lbs = _lbs_call(hgrn_lb_logits)
    n_c = bp + bd
    n_c_pad = -(-n_c // 8) * 8
    c_all = jnp.concatenate([c_prompt, c_sample, jnp.zeros((n_c_pad - n_c, d), F32)], axis=0)
    mod_all = _ada_call(c_all, w_ada, b_ada)
    bias_p = _bias_p_call(rel_bias)

    caches_t = [c.transpose(0, 1, 2, 4, 5, 3) for c in (cache_kv_w128, cache_kv_w512, cache_kv_w2048)]
    bias_s = [_bias_s_call(rel_bias, g, caches_t[g].shape[-1]) for g in range(A_GROUPS)]
    new_caches = [None] * A_GROUPS
    kv_p = [[] for _ in range(A_GROUPS)]
    s_p, s_s = [], []

    xp = x_prompt
    xs = x_sample.reshape(1, tm_s, d)
    hg_ranges = ((0, 1, F32),)
    sample_ranges = ((0, 9, F32), (9, 4, F32))
    for l in range(depth):
        mod_p = mod_all[l, 0:bp].reshape(bp, 1, N_MOD)
        mod_s = jnp.repeat(mod_all[l, bp:bp + bd], sd, axis=0).reshape(1, tm_s, N_MOD)
        lb = lbs[l:l + 1]
        nw = hgrn_norm_w[l:l + 1].astype(F32)
        g1, b1 = ln1_g[l:l + 1], ln1_b[l:l + 1]
        g2, b2 = ln2_g[l:l + 1], ln2_b[l:l + 1]

        qkv_groups = _qkv_p_call(xp, mod_p, w_in_b, l, tm=tm_p)
        (hg,) = _ln_mm_call(xp, mod_p, w_hg, l, ranges=hg_ranges, tm=tm_p, tn=4 * B_WIDTH, name="hg_proj_prompt")
        tails = _kv_tail_call(xp, mod_p, w_in_b, l)
        os, lses = [], []
        for g in range(A_GROUPS):
            o, lse = _attn_p_call(qkv_groups[g], bias_p, g)
            os.append(o)
            lses.append(lse)
            kv_p[g].append(tails[g])
        yb, st = _hgrn_call(hg, lb, nw, None, layer=l, seq_rows=HGRN_TILE, name="hgrn_prompt")
        s_p.append(st)
        xp = _merge_call(os, lses, yb, xp, mod_p, w_gate, wa_b, wb_b, wo_b, l, g1, b1, tm=tm_p,
                         name="merge_prompt")
        xp = _ffn_call(xp, mod_p, wup_b, wdn_b, l, g2, b2, tm=tm_p, name="ffn_prompt")

        qkv_s, hg_s = _ln_mm_call(xs, mod_s, w_in_b, l, ranges=sample_ranges, tm=tm_s, tn=512,
                                  name="in_proj_sample")
        qkv_s3 = qkv_s.reshape(bd, sd, n_qkv)
        os, lses = [], []
        for g in range(A_GROUPS):
            new_caches[g], o, lse = _attn_s_call(qkv_s3, caches_t[g], new_caches[g], bias_s[g], l, g)
            os.append(o.reshape(1, 1, tm_s, A_GROUP_WIDTH))
            lses.append(lse.reshape(1, 1, tm_s, LANES))
        yb_s, st_s = _hgrn_call(hg_s.reshape(bd, sd, 4 * B_WIDTH), lb, nw, state_hgrn, layer=l, seq_rows=sd,
                                name="hgrn_sample")
        s_s.append(st_s)
        xs = _merge_call(os, lses, yb_s.reshape(1, tm_s, B_WIDTH), xs, mod_s, w_gate, wa_b, wb_b, wo_b, l,
                         g1, b1, tm=tm_s, name="merge_sample")
        xs = _ffn_call(xs, mod_s, wup_b, wdn_b, l, g2, b2, tm=tm_s, name="ffn_sample")

    def rows_last(a):
        return a.transpose(0, 1, 2, 5, 3, 4)

    outs_kv_p = [rows_last(jnp.stack(kv_p[g]).reshape(depth, bp, 2, A_HEADS, A_HEAD_DIM, -1))
                 for g in range(A_GROUPS)]
    outs_kv_s = [rows_last(nc) for nc in new_caches]
    return (xp, xs.reshape(bd, sd, d),
            outs_kv_p[0], outs_kv_p[1], outs_kv_p[2], jnp.stack(s_p),
            outs_kv_s[0], outs_kv_s[1], outs_kv_s[2], jnp.stack(s_s))
```

```python
import functools
import math

import jax
import jax.numpy as jnp
import numpy as np
from jax import lax
from jax.experimental import pallas as pl
from jax.experimental.pallas import tpu as pltpu

F32 = jnp.float32
BF16 = jnp.bfloat16

D_MODEL = 1024
A_WINDOWS = (128, 512, 2048)
A_DILATIONS = (1, 4, 16)
A_GROUPS = 3
A_HEADS = 8
A_HEAD_DIM = 64
A_GROUP_WIDTH = A_HEADS * A_HEAD_DIM
A_QKV_WIDTH = A_GROUPS * A_GROUP_WIDTH
BAND = 128
NUM_BUCKETS = 32
MAX_DISTANCE = 2048
B_HEADS = 4
B_DIM = 128
B_WIDTH = B_HEADS * B_DIM
FF_DIM = 2816
N_MOD = 6 * D_MODEL
DEPTH = 4
ALPHA = (2 * DEPTH) ** 0.25
LN_EPS = 1e-5
RMS_EPS = 1e-6
NEG = -1e30
TINY = 1e-30
QK_SCALE = A_HEAD_DIM ** -0.5

LANES = 128
HEAD_PAIRS = A_GROUP_WIDTH // LANES
LANE_SLABS = A_GROUP_WIDTH // LANES
FF_SUBTILES = ((0, 512), (512, 512), (1024, 512), (1536, 512), (2048, 512), (2560, 256))
HGRN_TILE = 128
ATTN_QB = 512
DEC_ROWS = 8
PAD_ROWS = 16
VMEM_LIMIT = 56 * 1024 * 1024


def _cparams(sem):
    return pltpu.CompilerParams(dimension_semantics=sem, vmem_limit_bytes=VMEM_LIMIT)


def _sigmoid(x):
    return 1.0 / (1.0 + jnp.exp(-x))


def _silu(x):
    return x * _sigmoid(x)


def _ln(x):
    mu = jnp.mean(x, axis=-1, keepdims=True)
    xc = x - mu
    var = jnp.mean(xc * xc, axis=-1, keepdims=True)
    return xc * lax.rsqrt(var + LN_EPS)


def _dot(a, b):
    return jnp.dot(a, b, preferred_element_type=F32)


def _dot_nt(a, b):
    return lax.dot_general(a, b, (((1,), (1,)), ((), ())), preferred_element_type=F32)


def _modulated(x_ref, mod_ref, shift_chunk):
    sh = mod_ref[0, :, shift_chunk * D_MODEL:(shift_chunk + 1) * D_MODEL]
    sc = mod_ref[0, :, (shift_chunk + 1) * D_MODEL:(shift_chunk + 2) * D_MODEL]
    return (_ln(x_ref[0]) * (1.0 + sc) + sh).astype(BF16)


def _mod_spec(mod, tm, rank3):
    per_row = mod.shape[1] != 1
    if rank3:
        if per_row:
            return pl.BlockSpec((1, tm, N_MOD), lambda b, i, j: (b, i, 0))
        return pl.BlockSpec((1, 1, N_MOD), lambda b, i, j: (b, 0, 0))
    if per_row:
        return pl.BlockSpec((1, tm, N_MOD), lambda b, i: (b, i, 0))
    return pl.BlockSpec((1, 1, N_MOD), lambda b, i: (b, 0, 0))


def _lbs_kernel(lg_ref, out_ref):
    x = lg_ref[...]
    m = jnp.max(x, axis=0, keepdims=True)
    e = jnp.exp(x - m)
    p = e / jnp.sum(e, axis=0, keepdims=True)
    rows = []
    c = None
    for l in range(x.shape[0]):
        c = p[l:l + 1] if c is None else c + p[l:l + 1]
        rows.append(c - p[0:1])
    out_ref[...] = jnp.concatenate(rows, axis=0)


def _lbs_call(logits):
    return pl.pallas_call(
        _lbs_kernel, out_shape=jax.ShapeDtypeStruct(logits.shape, F32), name="hgrn_lbs",
    )(logits.astype(F32))


def _ada_kernel(c_ref, w_ref, b_ref, o_ref):
    o_ref[0] = _dot(c_ref[...].astype(BF16), w_ref[0].astype(BF16)) + b_ref[0]


def _ada_call(c_all, w_ada, b_ada):
    depth = w_ada.shape[0]
    rows = c_all.shape[0]
    tn = 512
    return pl.pallas_call(
        _ada_kernel,
        grid=(depth, N_MOD // tn),
        in_specs=[
            pl.BlockSpec((rows, D_MODEL), lambda l, j: (0, 0)),
            pl.BlockSpec((1, D_MODEL, tn), lambda l, j: (l, 0, j)),
            pl.BlockSpec((1, 1, tn), lambda l, j: (l, 0, j)),
        ],
        out_specs=pl.BlockSpec((1, rows, tn), lambda l, j: (l, 0, j)),
        out_shape=jax.ShapeDtypeStruct((depth, rows, N_MOD), F32),
        compiler_params=_cparams(("parallel", "parallel")),
        name="ada_mod",
    )(c_all, w_ada, b_ada.reshape(depth, 1, N_MOD))


def _t5_bucket_np(n):
    n = np.asarray(n, np.int32)
    max_exact = NUM_BUCKETS // 2
    nf = np.maximum(n.astype(np.float32), np.float32(max_exact))
    val = (np.log(nf / np.float32(max_exact)) / np.float32(math.log(MAX_DISTANCE / max_exact))
           * np.float32(NUM_BUCKETS - max_exact))
    large = max_exact + val.astype(np.int32)
    large = np.minimum(large, NUM_BUCKETS - 1)
    return np.where(n < max_exact, n, large).astype(np.int32)


def _bias_p_kernel(rb_ref, bk_ref, o_ref):
    g = pl.program_id(0)
    h = pl.program_id(2)
    bk = bk_ref[0, 0]
    acc = jnp.full(bk.shape, NEG, F32)
    for b in range(NUM_BUCKETS):
        acc = jnp.where(bk == b, rb_ref[b, g * A_HEADS + h], acc)
    o_ref[0, 0, 0] = acc


def _bias_p_call(rel_bias):
    qi = np.arange(BAND)[:, None]
    kj = np.arange(2 * BAND)[None, :]
    rel = qi + BAND - kj
    valid = (rel >= 0) & (rel <= BAND)
    bks = []
    for g in range(A_GROUPS):
        bk = np.where(valid, _t5_bucket_np(np.clip(rel, 0, BAND) * A_DILATIONS[g]), -1)
        bks.append(np.stack([bk, np.where(kj < BAND, -1, bk)]))
    bk_all = jnp.asarray(np.stack(bks).astype(np.int32))
    return pl.pallas_call(
        _bias_p_kernel,
        grid=(A_GROUPS, 2, A_HEADS),
        in_specs=[
            pl.BlockSpec(memory_space=pltpu.SMEM),
            pl.BlockSpec((1, 1, BAND, 2 * BAND), lambda g, f, h: (g, f, 0, 0)),
        ],
        out_specs=pl.BlockSpec((1, 1, 1, BAND, 2 * BAND), lambda g, f, h: (g, f, h, 0, 0)),
        out_shape=jax.ShapeDtypeStruct((A_GROUPS, 2, A_HEADS, BAND, 2 * BAND), F32),
        name="bias_prompt",
    )(rel_bias.astype(F32), bk_all)


def _bias_s_kernel(rb_ref, bk_ref, o_ref, *, g):
    h = pl.program_id(0)
    bk = bk_ref[...]
    acc = jnp.full(bk.shape, NEG, F32)
    for b in range(NUM_BUCKETS):
        acc = jnp.where(bk == b, rb_ref[b, g * A_HEADS + h], acc)
    o_ref[0] = acc


def _bias_s_call(rel_bias, g, lw):
    dil = A_DILATIONS[g]
    t = np.arange(DEC_ROWS)[:, None]
    col = np.arange(lw + LANES)[None, :]
    dist = lw + t - col
    ok = (dist >= 0) & (dist % dil == 0) & (dist // dil <= BAND) & (col < lw + DEC_ROWS)
    bk = np.where(ok, _t5_bucket_np(np.maximum(dist, 0)), -1).astype(np.int32)
    return pl.pallas_call(
        functools.partial(_bias_s_kernel, g=g),
        grid=(A_HEADS,),
        in_specs=[
            pl.BlockSpec(memory_space=pltpu.SMEM),
            pl.BlockSpec((DEC_ROWS, lw + LANES), lambda h: (0, 0)),
        ],
        out_specs=pl.BlockSpec((1, DEC_ROWS, lw + LANES), lambda h: (h, 0, 0)),
        out_shape=jax.ShapeDtypeStruct((A_HEADS, DEC_ROWS, lw + LANES), F32),
        name=f"bias_sample_g{g}",
    )(rel_bias.astype(F32), jnp.asarray(bk))


def _ln_mm_kernel(x_ref, mod_ref, w_ref, *refs, ranges):
    out_refs = refs[:len(ranges)]
    h_ref = refs[len(ranges)]
    j = pl.program_id(2)

    @pl.when(j == 0)
    def _():
        h_ref[...] = _modulated(x_ref, mod_ref, 0)

    for (start, cnt, _), o_ref in zip(ranges, out_refs):
        @pl.when((j >= start) & (j < start + cnt))
        def _(o_ref=o_ref):
            o_ref[0] = _dot(h_ref[...], w_ref[...]).astype(o_ref.dtype)


def _ln_mm_call(x, mod, w, layer, *, ranges, tm, tn, name):
    bsz, s, d = x.shape
    nj = sum(r[1] for r in ranges)
    assert nj * tn <= w.shape[2]
    out_specs = [
        pl.BlockSpec((1, tm, tn), lambda b, i, j, st=st, c=c: (b, i, jnp.clip(j - st, 0, c - 1)))
        for (st, c, _) in ranges
    ]
    out_shape = [jax.ShapeDtypeStruct((bsz, s, c * tn), dt) for (_, c, dt) in ranges]
    return pl.pallas_call(
        functools.partial(_ln_mm_kernel, ranges=ranges),
        grid=(bsz, s // tm, nj),
        in_specs=[
            pl.BlockSpec((1, tm, d), lambda b, i, j: (b, i, 0)),
            _mod_spec(mod, tm, True),
            pl.BlockSpec((None, d, tn), lambda b, i, j: (layer, 0, j)),
        ],
        out_specs=out_specs,
        out_shape=out_shape,
        scratch_shapes=[pltpu.VMEM((tm, d), BF16)],
        compiler_params=_cparams(("parallel", "parallel", "arbitrary")),
        name=name,
    )(x, mod, w)


def _qkv_p_kernel(x_ref, mod_ref, w_ref, o0_ref, o1_ref, o2_ref, scr_ref, *, tm):
    h = _modulated(x_ref, mod_ref, 0)
    for which in range(3):
        for g, o_ref in enumerate((o0_ref, o1_ref, o2_ref)):
            dil = A_DILATIONS[g]
            c0 = (which * A_GROUPS + g) * A_GROUP_WIDTH
            acc = _dot(h, w_ref[:, c0:c0 + A_GROUP_WIDTH])
            if which == 0:
                acc = acc * QK_SCALE
            cols = slice(which * A_GROUP_WIDTH, (which + 1) * A_GROUP_WIDTH)
            if dil == 1:
                o_ref[0, 0, :, cols] = acc.astype(BF16)
                continue
            slab = (which * A_GROUPS + g) % 2
            for lc in range(LANE_SLABS):
                scr_ref[slab, lc] = acc[:, lc * LANES:(lc + 1) * LANES]
            rows = tm // dil
            for r in range(dil):
                o_ref[0, r, :, cols] = jnp.concatenate(
                    [scr_ref[slab, lc, pl.ds(r, rows, stride=dil), :] for lc in range(LANE_SLABS)],
                    axis=1).astype(BF16)


def _qkv_p_call(x, mod, w_in, layer, *, tm):
    bsz, s, d = x.shape
    out_specs, out_shape = [], []
    for g in range(A_GROUPS):
        dil = A_DILATIONS[g]
        out_specs.append(pl.BlockSpec((1, dil, tm // dil, 3 * A_GROUP_WIDTH), lambda b, i: (b, 0, i, 0)))
        out_shape.append(jax.ShapeDtypeStruct((bsz, dil, s // dil, 3 * A_GROUP_WIDTH), BF16))
    return pl.pallas_call(
        functools.partial(_qkv_p_kernel, tm=tm),
        grid=(bsz, s // tm),
        in_specs=[
            pl.BlockSpec((1, tm, d), lambda b, i: (b, i, 0)),
            _mod_spec(mod, tm, False),
            pl.BlockSpec((None, d, 3 * A_QKV_WIDTH), lambda b, i: (layer, 0, 0), pipeline_mode=pl.Buffered(1)),
        ],
        out_specs=out_specs,
        out_shape=out_shape,
        scratch_shapes=[pltpu.VMEM((2, LANE_SLABS, tm, LANES), F32)],
        compiler_params=_cparams(("parallel", "arbitrary")),
        name="qkv_prompt",
    )(x, mod, w_in)


def _kv_tail_kernel(x_ref, mod_ref, wk_ref, wv_ref, o0_ref, o1_ref, o2_ref, acc_ref, *, tmk):
    i = pl.program_id(1)
    last = pl.num_programs(1) - 1
    h = _modulated(x_ref, mod_ref, 0)

    def kv_t(rows, kv, g):
        w_ref = wv_ref if kv else wk_ref
        n = rows.shape[0]
        acc_ref[0:n, :] = _dot(rows, w_ref[:, g * A_GROUP_WIDTH:(g + 1) * A_GROUP_WIDTH])
        return acc_ref[0:n, :].T

    for kv in range(2):
        o2_ref[0, kv] = kv_t(h, kv, 2)

    @pl.when(i == last)
    def _():
        for kv in range(2):
            o1_ref[0, kv] = kv_t(h, kv, 1)
            o0_ref[0, kv] = kv_t(h[tmk - A_WINDOWS[0]:, :], kv, 0)


def _kv_tail_call(x, mod, w_in, layer):
    bsz, s, d = x.shape
    tmk = A_WINDOWS[1]
    span = A_WINDOWS[2]
    nt = span // tmk
    off = (s - span) // tmk
    return pl.pallas_call(
        functools.partial(_kv_tail_kernel, tmk=tmk),
        grid=(bsz, nt),
        in_specs=[
            pl.BlockSpec((1, tmk, d), lambda b, i: (b, i + off, 0)),
            _mod_spec(mod, tmk, False),
            pl.BlockSpec((None, d, A_QKV_WIDTH), lambda b, i: (layer, 0, 1)),
            pl.BlockSpec((None, d, A_QKV_WIDTH), lambda b, i: (layer, 0, 2)),
        ],
        out_specs=[
            pl.BlockSpec((1, 2, A_GROUP_WIDTH, A_WINDOWS[0]), lambda b, i: (b, 0, 0, 0)),
            pl.BlockSpec((1, 2, A_GROUP_WIDTH, A_WINDOWS[1]), lambda b, i: (b, 0, 0, 0)),
            pl.BlockSpec((1, 2, A_GROUP_WIDTH, tmk), lambda b, i: (b, 0, 0, i)),
        ],
        out_shape=[jax.ShapeDtypeStruct((bsz, 2, A_GROUP_WIDTH, w), F32) for w in A_WINDOWS],
        scratch_shapes=[pltpu.VMEM((tmk, A_GROUP_WIDTH), F32)],
        compiler_params=_cparams(("parallel", "arbitrary")),
        name="kv_tail_prompt",
    )(x, mod, w_in, w_in)


def _attn_p_kernel(q_ref, kp_ref, ko_ref, vp_ref, vo_ref, bias_ref, o_ref, l_ref, *, qb):
    n = pl.program_id(2)
    kall = jnp.concatenate([kp_ref[0, 0], ko_ref[0, 0]], axis=0)
    vall = jnp.concatenate([vp_ref[0, 0], vo_ref[0, 0]], axis=0)
    lane = lax.broadcasted_iota(jnp.int32, (BAND, LANES), 1)
    lo = lane < A_HEAD_DIM
    first = (n == 0).astype(jnp.int32)
    for j in range(qb // BAND):
        q = q_ref[0, 0, j * BAND:(j + 1) * BAND, :]
        kc = kall[j * BAND:(j + 2) * BAND]
        vc = vall[j * BAND:(j + 2) * BAND]
        variant = first if j == 0 else 0
        o_tiles = []
        lse_tile = jnp.zeros((BAND, LANES), F32)
        for hp in range(HEAD_PAIRS):
            cs = slice(hp * LANES, (hp + 1) * LANES)
            qp, kp, vp = q[:, cs], kc[:, cs], vc[:, cs]
            res = []
            for half in range(2):
                h = 2 * hp + half
                qm = jnp.where(lo if half == 0 else jnp.logical_not(lo), qp, jnp.zeros_like(qp))
                s = _dot_nt(qm, kp) + bias_ref[variant, h]
                m = jnp.max(s, axis=-1, keepdims=True)
                p = jnp.exp(s - m)
                l = jnp.sum(p, axis=-1, keepdims=True)
                res.append(_dot(p.astype(BF16), vp) * (1.0 / l))
                lse_tile = jnp.where(lane == h, m + jnp.log(l), lse_tile)
            o_tiles.append(jnp.where(lo, res[0], res[1]))
        o_ref[0, 0, j * BAND:(j + 1) * BAND, :] = jnp.concatenate(o_tiles, axis=1).astype(o_ref.dtype)
        l_ref[0, 0, j * BAND:(j + 1) * BAND, :] = lse_tile


def _attn_p_call(qkv_g, bias_p, g):
    bsz, dil, m, _ = qkv_g.shape
    qb = min(ATTN_QB, m)
    sub = qb // BAND
    own = lambda which: pl.BlockSpec((1, 1, qb, A_GROUP_WIDTH), lambda b, r, n: (b, r, n, which))
    prev = lambda which: pl.BlockSpec((1, 1, BAND, A_GROUP_WIDTH),
                                      lambda b, r, n: (b, r, jnp.maximum(n * sub - 1, 0), which))
    return pl.pallas_call(
        functools.partial(_attn_p_kernel, qb=qb),
        grid=(bsz, dil, m // qb),
        in_specs=[
            own(0), prev(1), own(1), prev(2), own(2),
            pl.BlockSpec((None, 2, A_HEADS, BAND, 2 * BAND), lambda b, r, n: (g, 0, 0, 0, 0)),
        ],
        out_specs=[
            pl.BlockSpec((1, 1, qb, A_GROUP_WIDTH), lambda b, r, n: (b, r, n, 0)),
            pl.BlockSpec((1, 1, qb, LANES), lambda b, r, n: (b, r, n, 0)),
        ],
        out_shape=[
            jax.ShapeDtypeStruct((bsz, dil, m, A_GROUP_WIDTH), BF16),
            jax.ShapeDtypeStruct((bsz, dil, m, LANES), F32),
        ],
        compiler_params=_cparams(("parallel", "parallel", "arbitrary")),
        name=f"attn_prompt_g{g}",
    )(qkv_g, qkv_g, qkv_g, qkv_g, qkv_g, bias_p)


def _attn_s_kernel(q_ref, kn_ref, vn_ref, cache_ref, bias_ref, *refs, lw, nb, kvb, aliased):
    if aliased:
        refs = refs[1:]
    out_ref, o_ref, lse_ref, pc_ref, pn_ref = refs
    ph = pl.program_id(1)
    lane = lax.broadcasted_iota(jnp.int32, (A_HEAD_DIM, LANES), 1)
    lane8 = lax.broadcasted_iota(jnp.int32, (DEC_ROWS, LANES), 1)
    lo8 = lane8 < A_HEAD_DIM
    lane16 = lax.broadcasted_iota(jnp.int32, (PAD_ROWS, LANES), 1)
    lo16 = lane16 < A_HEAD_DIM
    zpad = lambda a: jnp.concatenate([a, jnp.zeros((PAD_ROWS - DEC_ROWS, a.shape[1]), F32)], axis=0)

    def transposed(new):
        return jnp.concatenate([new, jnp.zeros((LANES - DEC_ROWS, A_GROUP_WIDTH), F32)], axis=0).T

    def roll_write(bi, kvi, new_t):
        for h in range(A_HEADS):
            rolled = pltpu.roll(cache_ref[bi, kvi, h], lw - DEC_ROWS, 1)
            tail = pltpu.roll(new_t[h * A_HEAD_DIM:(h + 1) * A_HEAD_DIM, :], LANES - DEC_ROWS, 1)
            if lw > LANES:
                out_ref[bi, kvi, h, :, 0:lw - LANES] = rolled[:, 0:lw - LANES]
            out_ref[bi, kvi, h, :, lw - LANES:lw] = jnp.where(lane >= LANES - DEC_ROWS, tail,
                                                              rolled[:, lw - LANES:lw])

    def scores(bi, kvi, new_t):
        q = zpad(q_ref[bi] * QK_SCALE)
        lse_c = jnp.zeros((DEC_ROWS, LANES), F32)
        for hp in range(HEAD_PAIRS):
            qp = q[:, hp * LANES:(hp + 1) * LANES]
            kp = cache_ref[bi, kvi, 2 * hp:2 * hp + 2].reshape(LANES, lw).astype(BF16)
            kn = new_t[hp * LANES:(hp + 1) * LANES, :].astype(BF16)
            for half in range(2):
                h = 2 * hp + half
                qm = jnp.where(lo16 if half == 0 else jnp.logical_not(lo16), qp, 0.0).astype(BF16)
                s_c = _dot(qm, kp)[:DEC_ROWS] + bias_ref[h, :, 0:lw]
                s_n = _dot(qm, kn)[:DEC_ROWS] + bias_ref[h, :, lw:lw + LANES]
                m = jnp.maximum(jnp.max(s_c, axis=-1, keepdims=True), jnp.max(s_n, axis=-1, keepdims=True))
                p_c = jnp.exp(s_c - m)
                p_n = jnp.exp(s_n - m)
                l = jnp.sum(p_c, axis=-1, keepdims=True) + jnp.sum(p_n, axis=-1, keepdims=True)
                inv = 1.0 / l
                pc_ref[h] = zpad(p_c * inv)
                pn_ref[h] = zpad(p_n * inv)
                lse_c = jnp.where(lane8 == h, m + jnp.log(l), lse_c)
        lse_ref[bi] = lse_c

    def values(bi, kvi, new_t):
        o_tiles = []
        for hp in range(HEAD_PAIRS):
            vp = cache_ref[bi, kvi, 2 * hp:2 * hp + 2].reshape(LANES, lw).astype(BF16)
            vn = new_t[hp * LANES:(hp + 1) * LANES, :].astype(BF16)
            res = []
            for half in range(2):
                h = 2 * hp + half
                o = _dot_nt(pc_ref[h].astype(BF16), vp) + _dot_nt(pn_ref[h].astype(BF16), vn)
                res.append(o[:DEC_ROWS])
            o_tiles.append(jnp.where(lo8, res[0], res[1]))
        o_ref[bi] = jnp.concatenate(o_tiles, axis=1)

    def one_sequence(bi):
        if kvb == 2:
            k_t = transposed(kn_ref[bi])
            roll_write(bi, 0, k_t)
            scores(bi, 0, k_t)
            v_t = transposed(vn_ref[bi])
            roll_write(bi, 1, v_t)
            values(bi, 1, v_t)
        else:
            new_t = transposed(jnp.where(ph == 0, kn_ref[bi], vn_ref[bi]))
            roll_write(bi, 0, new_t)
            pl.when(ph == 0)(lambda: scores(bi, 0, new_t))
            pl.when(ph == 1)(lambda: values(bi, 0, new_t))

    if nb == 1:
        one_sequence(0)
    else:
        def body(bi, carry):
            one_sequence(bi)
            return carry
        lax.fori_loop(0, nb, body, 0)


SAMPLE_STEP = {128: (8, 2), 512: (2, 2), 2048: (1, 1)}


def _attn_s_call(qkv_s, cache_t, prev_out, bias_s, layer, g):
    depth, bd, _, _, _, lw = cache_t.shape
    assert lw % LANES == 0
    nb, kvb = SAMPLE_STEP.get(lw, (1, 1))
    nb = math.gcd(nb, bd)
    aliased = prev_out is not None
    cache_blk = (None, nb, kvb, A_HEADS, A_HEAD_DIM, lw)
    cache_idx = (lambda b, ph: (layer, b, ph, 0, 0, 0)) if kvb == 1 else (lambda b, ph: (layer, b, 0, 0, 0, 0))
    new_blk = (nb, DEC_ROWS, A_GROUP_WIDTH)
    in_specs = [
        pl.BlockSpec(new_blk, lambda b, ph: (b, 0, g)),
        pl.BlockSpec(new_blk, lambda b, ph: (b, 0, A_GROUPS + g)),
        pl.BlockSpec(new_blk, lambda b, ph: (b, 0, 2 * A_GROUPS + g)),
        pl.BlockSpec(cache_blk, cache_idx),
        pl.BlockSpec((A_HEADS, DEC_ROWS, lw + LANES), lambda b, ph: (0, 0, 0)),
    ]
    args = [qkv_s, qkv_s, qkv_s, cache_t, bias_s]
    aliases = {}
    if aliased:
        in_specs.append(pl.BlockSpec(memory_space=pl.ANY))
        args.append(prev_out)
        aliases = {5: 0}
    return pl.pallas_call(
        functools.partial(_attn_s_kernel, lw=lw, nb=nb, kvb=kvb, aliased=aliased),
        grid=(bd // nb, 2 // kvb),
        in_specs=in_specs,
        out_specs=[
            pl.BlockSpec(cache_blk, cache_idx),
            pl.BlockSpec(new_blk, lambda b, ph: (b, 0, 0)),
            pl.BlockSpec((nb, DEC_ROWS, LANES), lambda b, ph: (b, 0, 0)),
        ],
        out_shape=[
            jax.ShapeDtypeStruct(cache_t.shape, F32),
            jax.ShapeDtypeStruct((bd, DEC_ROWS, A_GROUP_WIDTH), F32),
            jax.ShapeDtypeStruct((bd, DEC_ROWS, LANES), F32),
        ],
        scratch_shapes=[
            pltpu.VMEM((A_HEADS, PAD_ROWS, lw), F32),
            pltpu.VMEM((A_HEADS, PAD_ROWS, LANES), F32),
        ],
        input_output_aliases=aliases,
        compiler_params=_cparams(("parallel", "arbitrary")),
        name=f"attn_sample_g{g}",
    )(*args)


def _hgrn_head(q, k, f, v, st, tt, span):
    ti = lax.broadcasted_iota(jnp.int32, (tt, tt), 0)
    si = lax.broadcasted_iota(jnp.int32, (tt, tt), 1)
    row = lax.broadcasted_iota(jnp.int32, (tt, B_DIM), 0)
    a = jnp.where(ti == si, _dot_nt(q.astype(BF16), k.astype(BF16)), 0.0)
    qe = q * f
    ke = k
    tot = f
    lg = 0
    while (1 << lg) < span:
        b = 1 << lg
        ab = _dot_nt(qe.astype(BF16), ke.astype(BF16))
        pair = ((ti >> lg) == (si >> lg) + 1) & ((ti >> (lg + 1)) == (si >> (lg + 1)))
        a = jnp.where(pair, ab, a)
        second = ((row >> lg) & 1) == 1
        prev_tot = pltpu.roll(tot, b, 0)
        next_tot = pltpu.roll(tot, tt - b, 0)
        qe = jnp.where(second, qe * prev_tot, qe)
        ke = jnp.where(second, ke, ke * next_tot)
        tot = tot * jnp.where(second, prev_tot, next_tot)
        lg += 1
    o = _dot(a.astype(BF16), v.astype(BF16)) + _dot_nt(qe.astype(BF16), st.astype(BF16))
    st_new = st * tot[0:1, :] + _dot(v.T.astype(BF16), ke.astype(BF16))
    return o, st_new


def _hgrn_kernel(qb_ref, fb_ref, ib_ref, gb_ref, lb_ref, nw_ref, *refs, tt, valid, has_s0):
    if has_s0:
        s0_ref, y_ref, s_out_ref, st_ref = refs
    else:
        y_ref, s_out_ref, st_ref = refs
    t = pl.program_id(1)
    nt = pl.num_programs(1)

    @pl.when(t == 0)
    def _():
        for hh in range(B_HEADS):
            if has_s0:
                st_ref[hh] = s0_ref[hh].T
            else:
                st_ref[hh] = jnp.zeros((B_DIM, B_DIM), F32)

    def load(ref):
        x = ref[0]
        if valid < tt:
            x = jnp.concatenate([x, jnp.zeros((tt - valid, x.shape[1]), F32)], axis=0)
        return x

    qb, fb, ib, gb = load(qb_ref), load(fb_ref), load(ib_ref), load(gb_ref)
    live = lax.broadcasted_iota(jnp.int32, (tt, B_DIM), 0) < valid
    for hh in range(B_HEADS):
        cs = slice(hh * B_DIM, (hh + 1) * B_DIM)
        lb = lb_ref[:, cs]
        sig = _sigmoid(fb[:, cs])
        f = jnp.maximum(lb + (1.0 - lb) * sig, TINY)
        k = (1.0 - lb) * (1.0 - sig)
        if valid < tt:
            f = jnp.where(live, f, 1.0)
            k = jnp.where(live, k, 0.0)
        span = min(tt, 1 << (valid - 1).bit_length())
        o, st_new = _hgrn_head(_silu(qb[:, cs]), k, f, ib[:, cs], st_ref[hh], tt, span)
        st_ref[hh] = st_new
        o = o * lax.rsqrt(jnp.mean(o * o, axis=-1, keepdims=True) + RMS_EPS) * nw_ref[:, cs]
        y = o * _silu(gb[:, cs])
        y_ref[0, :, cs] = y[:valid].astype(y_ref.dtype)

    @pl.when(t == nt - 1)
    def _():
        for hh in range(B_HEADS):
            s_out_ref[0, hh] = st_ref[hh].T


def _hgrn_call(hg, lb, nw, s0, *, layer, seq_rows, name):
    bsz, s, _ = hg.shape
    tt = HGRN_TILE
    valid = min(seq_rows, tt)
    nt = s // valid
    has_s0 = s0 is not None
    blk = (1, valid, B_WIDTH)
    in_specs = [pl.BlockSpec(blk, lambda b, t, c=c: (b, t, c)) for c in range(4)]
    in_specs += [pl.BlockSpec((1, B_WIDTH), lambda b, t: (0, 0))] * 2
    args = [hg, hg, hg, hg, lb, nw]
    if has_s0:
        in_specs.append(pl.BlockSpec((None, None, B_HEADS, B_DIM, B_DIM), lambda b, t: (layer, b, 0, 0, 0)))
        args.append(s0)
    return pl.pallas_call(
        functools.partial(_hgrn_kernel, tt=tt, valid=valid, has_s0=has_s0),
        grid=(bsz, nt),
        in_specs=in_specs,
        out_specs=[
            pl.BlockSpec(blk, lambda b, t: (b, t, 0)),
            pl.BlockSpec((1, B_HEADS, B_DIM, B_DIM), lambda b, t: (b, 0, 0, 0)),
        ],
        out_shape=[
            jax.ShapeDtypeStruct((bsz, s, B_WIDTH), BF16),
            jax.ShapeDtypeStruct((bsz, B_HEADS, B_DIM, B_DIM), F32),
        ],
        scratch_shapes=[pltpu.VMEM((B_HEADS, B_DIM, B_DIM), F32)],
        compiler_params=_cparams(("parallel", "arbitrary")),
        name=name,
    )(*args)


def _token_major(ref, dil, tm, scr_ref):
    if dil == 1:
        return ref[0, 0].astype(F32)
    rows = tm // dil
    slabs = ref.shape[3] // LANES
    for r in range(dil):
        blk = ref[0, r].astype(F32)
        for lc in range(slabs):
            scr_ref[lc, pl.ds(r, rows, stride=dil), :] = blk[:, lc * LANES:(lc + 1) * LANES]
    return jnp.concatenate([scr_ref[lc] for lc in range(slabs)], axis=1)


def _merge_kernel(o0_ref, o1_ref, o2_ref, l0_ref, l1_ref, l2_ref, yb_ref, x_ref, mod_ref,
                  wg_ref, wa_ref, wb_ref, wo_ref, g_ref, b_ref, out_ref, scr_ref, *, dils, tm):
    hi = lax.broadcasted_iota(jnp.int32, (2 * LANES, A_GROUP_WIDTH), 0)
    ci = lax.broadcasted_iota(jnp.int32, (2 * LANES, A_GROUP_WIDTH), 1)
    spread = jnp.where((ci >> 6) == (hi & (LANES - 1)), 1.0, 0.0).astype(BF16)

    def widen(lse):
        p0 = lse.astype(BF16)
        p1 = (lse - p0.astype(F32)).astype(BF16)
        return _dot(jnp.concatenate([p0, p1], axis=1), spread)

    os, ls = [], []
    for o_ref, l_ref, dil in zip((o0_ref, o1_ref, o2_ref), (l0_ref, l1_ref, l2_ref), dils):
        os.append(_token_major(o_ref, dil, tm, scr_ref))
        ls.append(widen(_token_major(l_ref, dil, tm, scr_ref)))
    m = jnp.maximum(jnp.maximum(ls[0], ls[1]), ls[2])
    es = [jnp.exp(l - m) for l in ls]
    ya = (es[0] * os[0] + es[1] * os[1] + es[2] * os[2]) / (es[0] + es[1] + es[2])
    br_a = _dot(ya.astype(BF16), wa_ref[...])
    br_b = _dot(yb_ref[0].astype(BF16), wb_ref[...])
    h = _modulated(x_ref, mod_ref, 0)
    gate_a = _dot(h, wg_ref[:, 0:D_MODEL])
    gate_b = _dot(h, wg_ref[:, D_MODEL:2 * D_MODEL])
    merged = _sigmoid(gate_a) * br_a + _sigmoid(gate_b) * br_b
    mix = _dot(merged.astype(BF16), wo_ref[...])
    gt = mod_ref[0, :, 2 * D_MODEL:3 * D_MODEL]
    z = ALPHA * x_ref[0] + (1.0 + gt) * mix
    out_ref[0] = _ln(z) * g_ref[...] + b_ref[...]


def _merge_call(os, lses, yb, x, mod, wg, wa, wb, wo, layer, ln_g, ln_b, *, tm, name):
    bsz, s, d = x.shape
    dils = tuple(o.shape[1] for o in os)
    tok = lambda w: pl.BlockSpec((1, tm, w), lambda b, i: (b, i, 0))
    res = lambda a: pl.BlockSpec((1, a.shape[1], tm // a.shape[1], a.shape[3]), lambda b, i: (b, 0, i, 0))
    full = lambda a: pl.BlockSpec(a.shape, lambda b, i: (0, 0))
    wsl = lambda a: pl.BlockSpec((None,) + a.shape[1:], lambda b, i: (layer, 0, 0), pipeline_mode=pl.Buffered(1))
    return pl.pallas_call(
        functools.partial(_merge_kernel, dils=dils, tm=tm),
        grid=(bsz, s // tm),
        in_specs=[res(a) for a in os] + [res(a) for a in lses] + [
            tok(B_WIDTH), tok(d), _mod_spec(mod, tm, False),
            wsl(wg), wsl(wa), wsl(wb), wsl(wo), full(ln_g), full(ln_b)],
        out_specs=tok(d),
        out_shape=jax.ShapeDtypeStruct((bsz, s, d), F32),
        scratch_shapes=[pltpu.VMEM((LANE_SLABS, tm, LANES), F32)],
        compiler_params=_cparams(("parallel", "parallel")),
        name=name,
    )(*os, *lses, yb, x, mod, wg, wa, wb, wo, ln_g, ln_b)


def _ffn_kernel(x_ref, mod_ref, wu_ref, wd_ref, g_ref, b_ref, out_ref, act_ref):
    h = _modulated(x_ref, mod_ref, 3)
    for st, sz in FF_SUBTILES:
        a = _dot(h, wu_ref[:, st:st + sz])
        gg = _dot(h, wu_ref[:, FF_DIM + st:FF_DIM + st + sz])
        act_ref[:, st:st + sz] = (_silu(a) * gg).astype(BF16)
    y = _dot(act_ref[...], wd_ref[...])
    gt = mod_ref[0, :, 5 * D_MODEL:6 * D_MODEL]
    z = ALPHA * x_ref[0] + (1.0 + gt) * y
    out_ref[0] = _ln(z) * g_ref[...] + b_ref[...]


def _ffn_call(x, mod, w_up, w_down, layer, ln_g, ln_b, *, tm, name):
    bsz, s, d = x.shape
    assert sum(sz for _, sz in FF_SUBTILES) == FF_DIM
    once = pl.Buffered(1)
    return pl.pallas_call(
        _ffn_kernel,
        grid=(bsz, s // tm),
        in_specs=[
            pl.BlockSpec((1, tm, d), lambda b, i: (b, i, 0)),
            _mod_spec(mod, tm, False),
            pl.BlockSpec((None, d, 2 * FF_DIM), lambda b, i: (layer, 0, 0), pipeline_mode=once),
            pl.BlockSpec((None, FF_DIM, d), lambda b, i: (layer, 0, 0), pipeline_mode=once),
            pl.BlockSpec((1, d), lambda b, i: (0, 0)),
            pl.BlockSpec((1, d), lambda b, i: (0, 0)),
        ],
        out_specs=pl.BlockSpec((1, tm, d), lambda b, i: (b, i, 0)),
        out_shape=jax.ShapeDtypeStruct((bsz, s, d), F32),
        scratch_shapes=[pltpu.VMEM((tm, FF_DIM), BF16)],
        compiler_params=_cparams(("parallel", "arbitrary")),
        name=name,
    )(x, mod, w_up, w_down, ln_g, ln_b)


def kernel(x_prompt, x_sample, c_prompt, c_sample, cache_kv_w128, cache_kv_w512, cache_kv_w2048, state_hgrn,
           w_in, w_branch_a, w_branch_b, w_out, hgrn_norm_w, hgrn_lb_logits, rel_bias,
           ffn_w_up, ffn_w_down, w_ada, b_ada, ln1_g, ln1_b, ln2_g, ln2_b):
    depth = w_in.shape[0]
    bp, sp, d = x_prompt.shape
    bd, sd, _ = x_sample.shape
    assert sd == DEC_ROWS and d == D_MODEL
    max_win = A_WINDOWS[-1]
    assert sp % max_win == 0
    tm_p = 1024
    tm_s = bd * sd
    n_qkv = 3 * A_QKV_WIDTH

    w_in_b = w_in.astype(BF16)
    w_hg = w_in_b[:, :, n_qkv:n_qkv + 4 * B_WIDTH]
    w_gate = w_in_b[:, :, n_qkv + 4 * B_WIDTH:]
    wa_b = w_branch_a.astype(BF16)
    wb_b = w_branch_b.astype(BF16)
    wo_b = w_out.astype(BF16)
    wup_b = ffn_w_up.astype(BF16)
    wdn_b = ffn_w_down.astype(BF16)

    lbs = _lbs_call(hgrn_lb_logits)
    n_c = bp + bd
    n_c_pad = -(-n_c // 8) * 8
    c_all = jnp.concatenate([c_prompt, c_sample, jnp.zeros((n_c_pad - n_c, d), F32)], axis=0)
    mod_all = _ada_call(c_all, w_ada, b_ada)
    bias_p = _bias_p_call(rel_bias)

    caches_t = [c.transpose(0, 1, 2, 4, 5, 3) for c in (cache_kv_w128, cache_kv_w512, cache_kv_w2048)]
    bias_s = [_bias_s_call(rel_bias, g, caches_t[g].shape[-1]) for g in range(A_GROUPS)]
    new_caches = [None] * A_GROUPS
    kv_p = [[] for _ in range(A_GROUPS)]
    s_p, s_s = [], []

    xp = x_prompt
    xs = x_sample.reshape(1, tm_s, d)
    hg_ranges = ((0, 1, F32),)
    sample_ranges = ((0, 9, F32), (9, 4, F32))
    for l in range(depth):
        mod_p = mod_all[l, 0:bp].reshape(bp, 1, N_MOD)
        mod_s = jnp.repeat(mod_all[l, bp:bp + bd], sd, axis=0).reshape(1, tm_s, N_MOD)
        lb = lbs[l:l + 1]
        nw = hgrn_norm_w[l:l + 1].astype(F32)
        g1, b1 = ln1_g[l:l + 1], ln1_b[l:l + 1]
        g2, b2 = ln2_g[l:l + 1], ln2_b[l:l + 1]

        qkv_groups = _qkv_p_call(xp, mod_p, w_in_b, l, tm=tm_p)
        (hg,) = _ln_mm_call(xp, mod_p, w_hg, l, ranges=hg_ranges, tm=tm_p, tn=4 * B_WIDTH, name="hg_proj_prompt")
        tails = _kv_tail_call(xp, mod_p, w_in_b, l)
        os, lses = [], []
        for g in range(A_GROUPS):
            o, lse = _attn_p_call(qkv_groups[g], bias_p, g)
            os.append(o)
            lses.append(lse)
            kv_p[g].append(tails[g])
        yb, st = _hgrn_call(hg, lb, nw, None, layer=l, seq_rows=HGRN_TILE, name="hgrn_prompt")
        s_p.append(st)
        xp = _merge_call(os, lses, yb, xp, mod_p, w_gate, wa_b, wb_b, wo_b, l, g1, b1, tm=tm_p,
                         name="merge_prompt")
        xp = _ffn_call(xp, mod_p, wup_b, wdn_b, l, g2, b2, tm=tm_p, name="ffn_prompt")

        qkv_s, hg_s = _ln_mm_call(xs, mod_s, w_in_b, l, ranges=sample_ranges, tm=tm_s, tn=512,
                                  name="in_proj_sample")
        qkv_s3 = qkv_s.reshape(bd, sd, n_qkv)
        os, lses = [], []
        for g in range(A_GROUPS):
            new_caches[g], o, lse = _attn_s_call(qkv_s3, caches_t[g], new_caches[g], bias_s[g], l, g)
            os.append(o.reshape(1, 1, tm_s, A_GROUP_WIDTH))
            lses.append(lse.reshape(1, 1, tm_s, LANES))
        yb_s, st_s = _hgrn_call(hg_s.reshape(bd, sd, 4 * B_WIDTH), lb, nw, state_hgrn, layer=l, seq_rows=sd,
                                name="hgrn_sample")
        s_s.append(st_s)
        xs = _merge_call(os, lses, yb_s.reshape(1, tm_s, B_WIDTH), xs, mod_s, w_gate, wa_b, wb_b, wo_b, l,
                         g1, b1, tm=tm_s, name="merge_sample")
        xs = _ffn_call(xs, mod_s, wup_b, wdn_b, l, g2, b2, tm=tm_s, name="ffn_sample")

    def rows_last(a):
        return a.transpose(0, 1, 2, 5, 3, 4)

    outs_kv_p = [rows_last(jnp.stack(kv_p[g]).reshape(depth, bp, 2, A_HEADS, A_HEAD_DIM, -1))
                 for g in range(A_GROUPS)]
    outs_kv_s = [rows_last(nc) for nc in new_caches]
    return (xp, xs.reshape(bd, sd, d),
            outs_kv_p[0], outs_kv_p[1], outs_kv_p[2], jnp.stack(s_p),
            outs_kv_s[0], outs_kv_s[1], outs_kv_s[2], jnp.stack(s_s))
```

```python
import functools
import math

import jax
import jax.numpy as jnp
import numpy as np
from jax import lax
from jax.experimental import pallas as pl
from jax.experimental.pallas import tpu as pltpu

F32 = jnp.float32
BF16 = jnp.bfloat16

D_MODEL = 1024
A_WINDOWS = (128, 512, 2048)
A_DILATIONS = (1, 4, 16)
A_GROUPS = 3
A_HEADS = 8
A_HEAD_DIM = 64
A_GROUP_WIDTH = A_HEADS * A_HEAD_DIM
A_QKV_WIDTH = A_GROUPS * A_GROUP_WIDTH
BAND = 128
NUM_BUCKETS = 32
MAX_DISTANCE = 2048
B_HEADS = 4
B_DIM = 128
B_WIDTH = B_HEADS * B_DIM
FF_DIM = 2816
N_MOD = 6 * D_MODEL
DEPTH = 4
ALPHA = (2 * DEPTH) ** 0.25
LN_EPS = 1e-5
RMS_EPS = 1e-6
NEG = -1e30
TINY = 1e-30
QK_SCALE = A_HEAD_DIM ** -0.5

LANES = 128
HEAD_PAIRS = A_GROUP_WIDTH // LANES
LANE_SLABS = A_GROUP_WIDTH // LANES
FF_SUBTILES = ((0, 512), (512, 512), (1024, 512), (1536, 512), (2048, 512), (2560, 256))
HGRN_TILE = 128
ATTN_QB = 512
DEC_ROWS = 8
PAD_ROWS = 16
VMEM_LIMIT = 56 * 1024 * 1024


def _cparams(sem):
    return pltpu.CompilerParams(dimension_semantics=sem, vmem_limit_bytes=VMEM_LIMIT)


def _sigmoid(x):
    return 1.0 / (1.0 + jnp.exp(-x))


def _silu(x):
    return x * _sigmoid(x)


def _ln(x):
    mu = jnp.mean(x, axis=-1, keepdims=True)
    xc = x - mu
    var = jnp.mean(xc * xc, axis=-1, keepdims=True)
    return xc * lax.rsqrt(var + LN_EPS)


def _dot(a, b):
    return jnp.dot(a, b, preferred_element_type=F32)


def _dot_nt(a, b):
    return lax.dot_general(a, b, (((1,), (1,)), ((), ())), preferred_element_type=F32)


def _modulated(x_ref, mod_ref, shift_chunk):
    sh = mod_ref[0, :, shift_chunk * D_MODEL:(shift_chunk + 1) * D_MODEL]
    sc = mod_ref[0, :, (shift_chunk + 1) * D_MODEL:(shift_chunk + 2) * D_MODEL]
    return (_ln(x_ref[0]) * (1.0 + sc) + sh).astype(BF16)


def _mod_spec(mod, tm, rank3):
    per_row = mod.shape[1] != 1
    if rank3:
        if per_row:
            return pl.BlockSpec((1, tm, N_MOD), lambda b, i, j: (b, i, 0))
        return pl.BlockSpec((1, 1, N_MOD), lambda b, i, j: (b, 0, 0))
    if per_row:
        return pl.BlockSpec((1, tm, N_MOD), lambda b, i: (b, i, 0))
    return pl.BlockSpec((1, 1, N_MOD), lambda b, i: (b, 0, 0))


def _lbs_kernel(lg_ref, out_ref):
    x = lg_ref[...]
    m = jnp.max(x, axis=0, keepdims=True)
    e = jnp.exp(x - m)
    p = e / jnp.sum(e, axis=0, keepdims=True)
    rows = []
    c = None
    for l in range(x.shape[0]):
        c = p[l:l + 1] if c is None else c + p[l:l + 1]
        rows.append(c - p[0:1])
    out_ref[...] = jnp.concatenate(rows, axis=0)


def _lbs_call(logits):
    return pl.pallas_call(
        _lbs_kernel, out_shape=jax.ShapeDtypeStruct(logits.shape, F32), name="hgrn_lbs",
    )(logits.astype(F32))


def _ada_kernel(c_ref, w_ref, b_ref, o_ref):
    o_ref[0] = _dot(c_ref[...].astype(BF16), w_ref[0].astype(BF16)) + b_ref[0]


def _ada_call(c_all, w_ada, b_ada):
    depth = w_ada.shape[0]
    rows = c_all.shape[0]
    tn = 512
    return pl.pallas_call(
        _ada_kernel,
        grid=(depth, N_MOD // tn),
        in_specs=[
            pl.BlockSpec((rows, D_MODEL), lambda l, j: (0, 0)),
            pl.BlockSpec((1, D_MODEL, tn), lambda l, j: (l, 0, j)),
            pl.BlockSpec((1, 1, tn), lambda l, j: (l, 0, j)),
        ],
        out_specs=pl.BlockSpec((1, rows, tn), lambda l, j: (l, 0, j)),
        out_shape=jax.ShapeDtypeStruct((depth, rows, N_MOD), F32),
        compiler_params=_cparams(("parallel", "parallel")),
        name="ada_mod",
    )(c_all, w_ada, b_ada.reshape(depth, 1, N_MOD))


def _t5_bucket_np(n):
    n = np.asarray(n, np.int32)
    max_exact = NUM_BUCKETS // 2
    nf = np.maximum(n.astype(np.float32), np.float32(max_exact))
    val = (np.log(nf / np.float32(max_exact)) / np.float32(math.log(MAX_DISTANCE / max_exact))
           * np.float32(NUM_BUCKETS - max_exact))
    large = max_exact + val.astype(np.int32)
    large = np.minimum(large, NUM_BUCKETS - 1)
    return np.where(n < max_exact, n, large).astype(np.int32)


def _bias_p_kernel(rb_ref, bk_ref, o_ref):
    g = pl.program_id(0)
    h = pl.program_id(2)
    bk = bk_ref[0, 0]
    acc = jnp.full(bk.shape, NEG, F32)
    for b in range(NUM_BUCKETS):
        acc = jnp.where(bk == b, rb_ref[b, g * A_HEADS + h], acc)
    o_ref[0, 0, 0] = acc


def _bias_p_call(rel_bias):
    qi = np.arange(BAND)[:, None]
    kj = np.arange(2 * BAND)[None, :]
    rel = qi + BAND - kj
    valid = (rel >= 0) & (rel <= BAND)
    bks = []
    for g in range(A_GROUPS):
        bk = np.where(valid, _t5_bucket_np(np.clip(rel, 0, BAND) * A_DILATIONS[g]), -1)
        bks.append(np.stack([bk, np.where(kj < BAND, -1, bk)]))
    bk_all = jnp.asarray(np.stack(bks).astype(np.int32))
    return pl.pallas_call(
        _bias_p_kernel,
        grid=(A_GROUPS, 2, A_HEADS),
        in_specs=[
            pl.BlockSpec(memory_space=pltpu.SMEM),
            pl.BlockSpec((1, 1, BAND, 2 * BAND), lambda g, f, h: (g, f, 0, 0)),
        ],
        out_specs=pl.BlockSpec((1, 1, 1, BAND, 2 * BAND), lambda g, f, h: (g, f, h, 0, 0)),
        out_shape=jax.ShapeDtypeStruct((A_GROUPS, 2, A_HEADS, BAND, 2 * BAND), F32),
        name="bias_prompt",
    )(rel_bias.astype(F32), bk_all)


def _bias_s_kernel(rb_ref, bk_ref, o_ref, *, g):
    h = pl.program_id(0)
    bk = bk_ref[...]
    acc = jnp.full(bk.shape, NEG, F32)
    for b in range(NUM_BUCKETS):
        acc = jnp.where(bk == b, rb_ref[b, g * A_HEADS + h], acc)
    o_ref[0] = acc


def _bias_s_call(rel_bias, g, lw):
    dil = A_DILATIONS[g]
    t = np.arange(DEC_ROWS)[:, None]
    col = np.arange(lw + LANES)[None, :]
    dist = lw + t - col
    ok = (dist >= 0) & (dist % dil == 0) & (dist // dil <= BAND) & (col < lw + DEC_ROWS)
    bk = np.where(ok, _t5_bucket_np(np.maximum(dist, 0)), -1).astype(np.int32)
    return pl.pallas_call(
        functools.partial(_bias_s_kernel, g=g),
        grid=(A_HEADS,),
        in_specs=[
            pl.BlockSpec(memory_space=pltpu.SMEM),
            pl.BlockSpec((DEC_ROWS, lw + LANES), lambda h: (0, 0)),
        ],
        out_specs=pl.BlockSpec((1, DEC_ROWS, lw + LANES), lambda h: (h, 0, 0)),
        out_shape=jax.ShapeDtypeStruct((A_HEADS, DEC_ROWS, lw + LANES), F32),
        name=f"bias_sample_g{g}",
    )(rel_bias.astype(F32), jnp.asarray(bk))


def _ln_mm_kernel(x_ref, mod_ref, w_ref, *refs, ranges):
    out_refs = refs[:len(ranges)]
    h_ref = refs[len(ranges)]
    j = pl.program_id(2)

    @pl.when(j == 0)
    def _():
        h_ref[...] = _modulated(x_ref, mod_ref, 0)

    for (start, cnt, _), o_ref in zip(ranges, out_refs):
        @pl.when((j >= start) & (j < start + cnt))
        def _(o_ref=o_ref):
            o_ref[0] = _dot(h_ref[...], w_ref[...]).astype(o_ref.dtype)


def _ln_mm_call(x, mod, w, layer, *, ranges, tm, tn, name):
    bsz, s, d = x.shape
    nj = sum(r[1] for r in ranges)
    assert nj * tn <= w.shape[2]
    out_specs = [
        pl.BlockSpec((1, tm, tn), lambda b, i, j, st=st, c=c: (b, i, jnp.clip(j - st, 0, c - 1)))
        for (st, c, _) in ranges
    ]
    out_shape = [jax.ShapeDtypeStruct((bsz, s, c * tn), dt) for (_, c, dt) in ranges]
    return pl.pallas_call(
        functools.partial(_ln_mm_kernel, ranges=ranges),
        grid=(bsz, s // tm, nj),
        in_specs=[
            pl.BlockSpec((1, tm, d), lambda b, i, j: (b, i, 0)),
            _mod_spec(mod, tm, True),
            pl.BlockSpec((None, d, tn), lambda b, i, j: (layer, 0, j)),
        ],
        out_specs=out_specs,
        out_shape=out_shape,
        scratch_shapes=[pltpu.VMEM((tm, d), BF16)],
        compiler_params=_cparams(("parallel", "parallel", "arbitrary")),
        name=name,
    )(x, mod, w)


def _qkv_p_kernel(x_ref, mod_ref, w_ref, o0_ref, o1_ref, o2_ref, scr_ref, *, tm):
    h = _modulated(x_ref, mod_ref, 0)
    for which in range(3):
        for g, o_ref in enumerate((o0_ref, o1_ref, o2_ref)):
            dil = A_DILATIONS[g]
            c0 = (which * A_GROUPS + g) * A_GROUP_WIDTH
            acc = _dot(h, w_ref[:, c0:c0 + A_GROUP_WIDTH])
            if which == 0:
                acc = acc * QK_SCALE
            cols = slice(which * A_GROUP_WIDTH, (which + 1) * A_GROUP_WIDTH)
            if dil == 1:
                o_ref[0, 0, :, cols] = acc.astype(BF16)
                continue
            slab = (which * A_GROUPS + g) % 2
            for lc in range(LANE_SLABS):
                scr_ref[slab, lc] = acc[:, lc * LANES:(lc + 1) * LANES]
            rows = tm // dil
            for r in range(dil):
                o_ref[0, r, :, cols] = jnp.concatenate(
                    [scr_ref[slab, lc, pl.ds(r, rows, stride=dil), :] for lc in range(LANE_SLABS)],
                    axis=1).astype(BF16)


def _qkv_p_call(x, mod, w_in, layer, *, tm):
    bsz, s, d = x.shape
    out_specs, out_shape = [], []
    for g in range(A_GROUPS):
        dil = A_DILATIONS[g]
        out_specs.append(pl.BlockSpec((1, dil, tm // dil, 3 * A_GROUP_WIDTH), lambda b, i: (b, 0, i, 0)))
        out_shape.append(jax.ShapeDtypeStruct((bsz, dil, s // dil, 3 * A_GROUP_WIDTH), BF16))
    return pl.pallas_call(
        functools.partial(_qkv_p_kernel, tm=tm),
        grid=(bsz, s // tm),
        in_specs=[
            pl.BlockSpec((1, tm, d), lambda b, i: (b, i, 0)),
            _mod_spec(mod, tm, False),
            pl.BlockSpec((None, d, 3 * A_QKV_WIDTH), lambda b, i: (layer, 0, 0), pipeline_mode=pl.Buffered(1)),
        ],
        out_specs=out_specs,
        out_shape=out_shape,
        scratch_shapes=[pltpu.VMEM((2, LANE_SLABS, tm, LANES), F32)],
        compiler_params=_cparams(("parallel", "arbitrary")),
        name="qkv_prompt",
    )(x, mod, w_in)


def _kv_tail_kernel(x_ref, mod_ref, wk_ref, wv_ref, o0_ref, o1_ref, o2_ref, acc_ref, *, tmk):
    i = pl.program_id(1)
    last = pl.num_programs(1) - 1
    h = _modulated(x_ref, mod_ref, 0)

    def kv_t(rows, kv, g):
        w_ref = wv_ref if kv else wk_ref
        n = rows.shape[0]
        acc_ref[0:n, :] = _dot(rows, w_ref[:, g * A_GROUP_WIDTH:(g + 1) * A_GROUP_WIDTH])
        return acc_ref[0:n, :].T

    for kv in range(2):
        o2_ref[0, kv] = kv_t(h, kv, 2)

    @pl.when(i == last)
    def _():
        for kv in range(2):
            o1_ref[0, kv] = kv_t(h, kv, 1)
            o0_ref[0, kv] = kv_t(h[tmk - A_WINDOWS[0]:, :], kv, 0)


def _kv_tail_call(x, mod, w_in, layer):
    bsz, s, d = x.shape
    tmk = A_WINDOWS[1]
    span = A_WINDOWS[2]
    nt = span // tmk
    off = (s - span) // tmk
    return pl.pallas_call(
        functools.partial(_kv_tail_kernel, tmk=tmk),
        grid=(bsz, nt),
        in_specs=[
            pl.BlockSpec((1, tmk, d), lambda b, i: (b, i + off, 0)),
            _mod_spec(mod, tmk, False),
            pl.BlockSpec((None, d, A_QKV_WIDTH), lambda b, i: (layer, 0, 1)),
            pl.BlockSpec((None, d, A_QKV_WIDTH), lambda b, i: (layer, 0, 2)),
        ],
        out_specs=[
            pl.BlockSpec((1, 2, A_GROUP_WIDTH, A_WINDOWS[0]), lambda b, i: (b, 0, 0, 0)),
            pl.BlockSpec((1, 2, A_GROUP_WIDTH, A_WINDOWS[1]), lambda b, i: (b, 0, 0, 0)),
            pl.BlockSpec((1, 2, A_GROUP_WIDTH, tmk), lambda b, i: (b, 0, 0, i)),
        ],
        out_shape=[jax.ShapeDtypeStruct((bsz, 2, A_GROUP_WIDTH, w), F32) for w in A_WINDOWS],
        scratch_shapes=[pltpu.VMEM((tmk, A_GROUP_WIDTH), F32)],
        compiler_params=_cparams(("parallel", "arbitrary")),
        name="kv_tail_prompt",
    )(x, mod, w_in, w_in)


def _attn_p_kernel(q_ref, kp_ref, ko_ref, vp_ref, vo_ref, bias_ref, o_ref, l_ref, *, qb):
    n = pl.program_id(2)
    kall = jnp.concatenate([kp_ref[0, 0], ko_ref[0, 0]], axis=0)
    vall = jnp.concatenate([vp_ref[0, 0], vo_ref[0, 0]], axis=0)
    lane = lax.broadcasted_iota(jnp.int32, (BAND, LANES), 1)
    lo = lane < A_HEAD_DIM
    first = (n == 0).astype(jnp.int32)
    for j in range(qb // BAND):
        q = q_ref[0, 0, j * BAND:(j + 1) * BAND, :]
        kc = kall[j * BAND:(j + 2) * BAND]
        vc = vall[j * BAND:(j + 2) * BAND]
        variant = first if j == 0 else 0
        o_tiles = []
        lse_tile = jnp.zeros((BAND, LANES), F32)
        for hp in range(HEAD_PAIRS):
            cs = slice(hp * LANES, (hp + 1) * LANES)
            qp, kp = q[:, cs], kc[:, cs]
            vp = jnp.concatenate([vc[:, cs], jnp.ones((2 * BAND, LANES), BF16)], axis=1)
            res = []
            for half in range(2):
                h = 2 * hp + half
                qm = jnp.where(lo if half == 0 else jnp.logical_not(lo), qp, jnp.zeros_like(qp))
                s = _dot_nt(qm, kp) + bias_ref[variant, h]
                m = jnp.max(s, axis=-1, keepdims=True)
                p = jnp.exp(s - m)
                ov = _dot(p.astype(BF16), vp)
                l = ov[:, LANES:2 * LANES]
                res.append(ov[:, 0:LANES] * (1.0 / l))
                lse_tile = jnp.where(lane == h, m + jnp.log(l), lse_tile)
            o_tiles.append(jnp.where(lo, res[0], res[1]))
        o_ref[0, 0, j * BAND:(j + 1) * BAND, :] = jnp.concatenate(o_tiles, axis=1).astype(o_ref.dtype)
        l_ref[0, 0, j * BAND:(j + 1) * BAND, :] = lse_tile


def _attn_p_call(qkv_g, bias_p, g):
    bsz, dil, m, _ = qkv_g.shape
    qb = min(ATTN_QB, m)
    sub = qb // BAND
    own = lambda which: pl.BlockSpec((1, 1, qb, A_GROUP_WIDTH), lambda b, r, n: (b, r, n, which))
    prev = lambda which: pl.BlockSpec((1, 1, BAND, A_GROUP_WIDTH),
                                      lambda b, r, n: (b, r, jnp.maximum(n * sub - 1, 0), which))
    return pl.pallas_call(
        functools.partial(_attn_p_kernel, qb=qb),
        grid=(bsz, dil, m // qb),
        in_specs=[
            own(0), prev(1), own(1), prev(2), own(2),
            pl.BlockSpec((None, 2, A_HEADS, BAND, 2 * BAND), lambda b, r, n: (g, 0, 0, 0, 0)),
        ],
        out_specs=[
            pl.BlockSpec((1, 1, qb, A_GROUP_WIDTH), lambda b, r, n: (b, r, n, 0)),
            pl.BlockSpec((1, 1, qb, LANES), lambda b, r, n: (b, r, n, 0)),
        ],
        out_shape=[
            jax.ShapeDtypeStruct((bsz, dil, m, A_GROUP_WIDTH), BF16),
            jax.ShapeDtypeStruct((bsz, dil, m, LANES), F32),
        ],
        compiler_params=_cparams(("parallel", "parallel", "arbitrary")),
        name=f"attn_prompt_g{g}",
    )(qkv_g, qkv_g, qkv_g, qkv_g, qkv_g, bias_p)


def _attn_s_kernel(q_ref, kn_ref, vn_ref, cache_ref, bias_ref, *refs, lw, nb, kvb, aliased):
    if aliased:
        refs = refs[1:]
    out_ref, o_ref, lse_ref, pc_ref, pn_ref = refs
    ph = pl.program_id(1)
    lane = lax.broadcasted_iota(jnp.int32, (A_HEAD_DIM, LANES), 1)
    lane8 = lax.broadcasted_iota(jnp.int32, (DEC_ROWS, LANES), 1)
    lo8 = lane8 < A_HEAD_DIM
    lane16 = lax.broadcasted_iota(jnp.int32, (PAD_ROWS, LANES), 1)
    lo16 = lane16 < A_HEAD_DIM
    zpad = lambda a: jnp.concatenate([a, jnp.zeros((PAD_ROWS - DEC_ROWS, a.shape[1]), F32)], axis=0)

    def transposed(new):
        return jnp.concatenate([new, jnp.zeros((LANES - DEC_ROWS, A_GROUP_WIDTH), F32)], axis=0).T

    def roll_write(bi, kvi, new_t):
        for h in range(A_HEADS):
            rolled = pltpu.roll(cache_ref[bi, kvi, h], lw - DEC_ROWS, 1)
            tail = pltpu.roll(new_t[h * A_HEAD_DIM:(h + 1) * A_HEAD_DIM, :], LANES - DEC_ROWS, 1)
            if lw > LANES:
                out_ref[bi, kvi, h, :, 0:lw - LANES] = rolled[:, 0:lw - LANES]
            out_ref[bi, kvi, h, :, lw - LANES:lw] = jnp.where(lane >= LANES - DEC_ROWS, tail,
                                                              rolled[:, lw - LANES:lw])

    def scores(bi, kvi, new_t):
        q = zpad(q_ref[bi] * QK_SCALE)
        lse_c = jnp.zeros((DEC_ROWS, LANES), F32)
        for hp in range(HEAD_PAIRS):
            qp = q[:, hp * LANES:(hp + 1) * LANES]
            kp = cache_ref[bi, kvi, 2 * hp:2 * hp + 2].reshape(LANES, lw).astype(BF16)
            kn = new_t[hp * LANES:(hp + 1) * LANES, :].astype(BF16)
            for half in range(2):
                h = 2 * hp + half
                qm = jnp.where(lo16 if half == 0 else jnp.logical_not(lo16), qp, 0.0).astype(BF16)
                s_c = _dot(qm, kp)[:DEC_ROWS] + bias_ref[h, :, 0:lw]
                s_n = _dot(qm, kn)[:DEC_ROWS] + bias_ref[h, :, lw:lw + LANES]
                m = jnp.maximum(jnp.max(s_c, axis=-1, keepdims=True), jnp.max(s_n, axis=-1, keepdims=True))
                p_c = jnp.exp(s_c - m)
                p_n = jnp.exp(s_n - m)
                l = jnp.sum(p_c, axis=-1, keepdims=True) + jnp.sum(p_n, axis=-1, keepdims=True)
                inv = 1.0 / l
                pc_ref[h] = zpad(p_c * inv)
                pn_ref[h] = zpad(p_n * inv)
                lse_c = jnp.where(lane8 == h, m + jnp.log(l), lse_c)
        lse_ref[bi] = lse_c

    def values(bi, kvi, new_t):
        o_tiles = []
        for hp in range(HEAD_PAIRS):
            vp = cache_ref[bi, kvi, 2 * hp:2 * hp + 2].reshape(LANES, lw).astype(BF16)
            vn = new_t[hp * LANES:(hp + 1) * LANES, :].astype(BF16)
            res = []
            for half in range(2):
                h = 2 * hp + half
                o = _dot_nt(pc_ref[h].astype(BF16), vp) + _dot_nt(pn_ref[h].astype(BF16), vn)
                res.append(o[:DEC_ROWS])
            o_tiles.append(jnp.where(lo8, res[0], res[1]))
        o_ref[bi] = jnp.concatenate(o_tiles, axis=1)

    def one_sequence(bi):
        if kvb == 2:
            k_t = transposed(kn_ref[bi])
            roll_write(bi, 0, k_t)
            scores(bi, 0, k_t)
            v_t = transposed(vn_ref[bi])
            roll_write(bi, 1, v_t)
            values(bi, 1, v_t)
        else:
            new_t = transposed(jnp.where(ph == 0, kn_ref[bi], vn_ref[bi]))
            roll_write(bi, 0, new_t)
            pl.when(ph == 0)(lambda: scores(bi, 0, new_t))
            pl.when(ph == 1)(lambda: values(bi, 0, new_t))

    if nb == 1:
        one_sequence(0)
    else:
        def body(bi, carry):
            one_sequence(bi)
            return carry
        lax.fori_loop(0, nb, body, 0)


SAMPLE_STEP = {128: (8, 2), 512: (2, 2), 2048: (1, 1)}


def _attn_s_call(qkv_s, cache_t, prev_out, bias_s, layer, g):
    depth, bd, _, _, _, lw = cache_t.shape
    assert lw % LANES == 0
    nb, kvb = SAMPLE_STEP.get(lw, (1, 1))
    nb = math.gcd(nb, bd)
    aliased = prev_out is not None
    cache_blk = (None, nb, kvb, A_HEADS, A_HEAD_DIM, lw)
    cache_idx = (lambda b, ph: (layer, b, ph, 0, 0, 0)) if kvb == 1 else (lambda b, ph: (layer, b, 0, 0, 0, 0))
    new_blk = (nb, DEC_ROWS, A_GROUP_WIDTH)
    in_specs = [
        pl.BlockSpec(new_blk, lambda b, ph: (b, 0, g)),
        pl.BlockSpec(new_blk, lambda b, ph: (b, 0, A_GROUPS + g)),
        pl.BlockSpec(new_blk, lambda b, ph: (b, 0, 2 * A_GROUPS + g)),
        pl.BlockSpec(cache_blk, cache_idx),
        pl.BlockSpec((A_HEADS, DEC_ROWS, lw + LANES), lambda b, ph: (0, 0, 0)),
    ]
    args = [qkv_s, qkv_s, qkv_s, cache_t, bias_s]
    aliases = {}
    if aliased:
        in_specs.append(pl.BlockSpec(memory_space=pl.ANY))
        args.append(prev_out)
        aliases = {5: 0}
    return pl.pallas_call(
        functools.partial(_attn_s_kernel, lw=lw, nb=nb, kvb=kvb, aliased=aliased),
        grid=(bd // nb, 2 // kvb),
        in_specs=in_specs,
        out_specs=[
            pl.BlockSpec(cache_blk, cache_idx),
            pl.BlockSpec(new_blk, lambda b, ph: (b, 0, 0)),
            pl.BlockSpec((nb, DEC_ROWS, LANES), lambda b, ph: (b, 0, 0)),
        ],
        out_shape=[
            jax.ShapeDtypeStruct(cache_t.shape, F32),
            jax.ShapeDtypeStruct((bd, DEC_ROWS, A_GROUP_WIDTH), F32),
            jax.ShapeDtypeStruct((bd, DEC_ROWS, LANES), F32),
        ],
        scratch_shapes=[
            pltpu.VMEM((A_HEADS, PAD_ROWS, lw), F32),
            pltpu.VMEM((A_HEADS, PAD_ROWS, LANES), F32),
        ],
        input_output_aliases=aliases,
        compiler_params=_cparams(("parallel", "arbitrary")),
        name=f"attn_sample_g{g}",
    )(*args)


def _hgrn_head(q, k, f, v, st, tt, span):
    ti = lax.broadcasted_iota(jnp.int32, (tt, tt), 0)
    si = lax.broadcasted_iota(jnp.int32, (tt, tt), 1)
    row = lax.broadcasted_iota(jnp.int32, (tt, B_DIM), 0)
    a = jnp.where(ti == si, _dot_nt(q.astype(BF16), k.astype(BF16)), 0.0)
    qe = q * f
    ke = k
    tot = f
    lg = 0
    while (1 << lg) < span:
        b = 1 << lg
        ab = _dot_nt(qe.astype(BF16), ke.astype(BF16))
        pair = ((ti >> lg) == (si >> lg) + 1) & ((ti >> (lg + 1)) == (si >> (lg + 1)))
        a = jnp.where(pair, ab, a)
        second = ((row >> lg) & 1) == 1
        prev_tot = pltpu.roll(tot, b, 0)
        next_tot = pltpu.roll(tot, tt - b, 0)
        qe = jnp.where(second, qe * prev_tot, qe)
        ke = jnp.where(second, ke, ke * next_tot)
        tot = tot * jnp.where(second, prev_tot, next_tot)
        lg += 1
    o = _dot(a.astype(BF16), v.astype(BF16)) + _dot_nt(qe.astype(BF16), st.astype(BF16))
    st_new = st * tot[0:1, :] + _dot(v.T.astype(BF16), ke.astype(BF16))
    return o, st_new


def _hgrn_kernel(qb_ref, fb_ref, ib_ref, gb_ref, lb_ref, nw_ref, *refs, tt, valid, has_s0):
    if has_s0:
        s0_ref, y_ref, s_out_ref, st_ref = refs
    else:
        y_ref, s_out_ref, st_ref = refs
    t = pl.program_id(1)
    nt = pl.num_programs(1)

    @pl.when(t == 0)
    def _():
        for hh in range(B_HEADS):
            if has_s0:
                st_ref[hh] = s0_ref[hh].T
            else:
                st_ref[hh] = jnp.zeros((B_DIM, B_DIM), F32)

    def load(ref):
        x = ref[0]
        if valid < tt:
            x = jnp.concatenate([x, jnp.zeros((tt - valid, x.shape[1]), F32)], axis=0)
        return x

    qb, fb, ib, gb = load(qb_ref), load(fb_ref), load(ib_ref), load(gb_ref)
    live = lax.broadcasted_iota(jnp.int32, (tt, B_DIM), 0) < valid
    for hh in range(B_HEADS):
        cs = slice(hh * B_DIM, (hh + 1) * B_DIM)
        lb = lb_ref[:, cs]
        sig = _sigmoid(fb[:, cs])
        f = jnp.maximum(lb + (1.0 - lb) * sig, TINY)
        k = (1.0 - lb) * (1.0 - sig)
        if valid < tt:
            f = jnp.where(live, f, 1.0)
            k = jnp.where(live, k, 0.0)
        span = min(tt, 1 << (valid - 1).bit_length())
        o, st_new = _hgrn_head(_silu(qb[:, cs]), k, f, ib[:, cs], st_ref[hh], tt, span)
        st_ref[hh] = st_new
        o = o * lax.rsqrt(jnp.mean(o * o, axis=-1, keepdims=True) + RMS_EPS) * nw_ref[:, cs]
        y = o * _silu(gb[:, cs])
        y_ref[0, :, cs] = y[:valid].astype(y_ref.dtype)

    @pl.when(t == nt - 1)
    def _():
        for hh in range(B_HEADS):
            s_out_ref[0, hh] = st_ref[hh].T


def _hgrn_call(hg, lb, nw, s0, *, layer, seq_rows, name):
    bsz, s, _ = hg.shape
    tt = HGRN_TILE
    valid = min(seq_rows, tt)
    nt = s // valid
    has_s0 = s0 is not None
    blk = (1, valid, B_WIDTH)
    in_specs = [pl.BlockSpec(blk, lambda b, t, c=c: (b, t, c)) for c in range(4)]
    in_specs += [pl.BlockSpec((1, B_WIDTH), lambda b, t: (0, 0))] * 2
    args = [hg, hg, hg, hg, lb, nw]
    if has_s0:
        in_specs.append(pl.BlockSpec((None, None, B_HEADS, B_DIM, B_DIM), lambda b, t: (layer, b, 0, 0, 0)))
        args.append(s0)
    return pl.pallas_call(
        functools.partial(_hgrn_kernel, tt=tt, valid=valid, has_s0=has_s0),
        grid=(bsz, nt),
        in_specs=in_specs,
        out_specs=[
            pl.BlockSpec(blk, lambda b, t: (b, t, 0)),
            pl.BlockSpec((1, B_HEADS, B_DIM, B_DIM), lambda b, t: (b, 0, 0, 0)),
        ],
        out_shape=[
            jax.ShapeDtypeStruct((bsz, s, B_WIDTH), BF16),
            jax.ShapeDtypeStruct((bsz, B_HEADS, B_DIM, B_DIM), F32),
        ],
        scratch_shapes=[pltpu.VMEM((B_HEADS, B_DIM, B_DIM), F32)],
        compiler_params=_cparams(("parallel", "arbitrary")),
        name=name,
    )(*args)


def _token_major(ref, dil, tm, scr_ref):
    if dil == 1:
        return ref[0, 0].astype(F32)
    rows = tm // dil
    slabs = ref.shape[3] // LANES
    for r in range(dil):
        blk = ref[0, r].astype(F32)
        for lc in range(slabs):
            scr_ref[lc, pl.ds(r, rows, stride=dil), :] = blk[:, lc * LANES:(lc + 1) * LANES]
    return jnp.concatenate([scr_ref[lc] for lc in range(slabs)], axis=1)


def _merge_kernel(o0_ref, o1_ref, o2_ref, l0_ref, l1_ref, l2_ref, yb_ref, x_ref, mod_ref,
                  wg_ref, wa_ref, wb_ref, wo_ref, g_ref, b_ref, out_ref, scr_ref, *, dils, tm):
    hi = lax.broadcasted_iota(jnp.int32, (2 * LANES, A_GROUP_WIDTH), 0)
    ci = lax.broadcasted_iota(jnp.int32, (2 * LANES, A_GROUP_WIDTH), 1)
    spread = jnp.where((ci >> 6) == (hi & (LANES - 1)), 1.0, 0.0).astype(BF16)

    def widen(lse):
        p0 = lse.astype(BF16)
        p1 = (lse - p0.astype(F32)).astype(BF16)
        return _dot(jnp.concatenate([p0, p1], axis=1), spread)

    os, ls = [], []
    for o_ref, l_ref, dil in zip((o0_ref, o1_ref, o2_ref), (l0_ref, l1_ref, l2_ref), dils):
        os.append(_token_major(o_ref, dil, tm, scr_ref))
        ls.append(widen(_token_major(l_ref, dil, tm, scr_ref)))
    m = jnp.maximum(jnp.maximum(ls[0], ls[1]), ls[2])
    es = [jnp.exp(l - m) for l in ls]
    ya = (es[0] * os[0] + es[1] * os[1] + es[2] * os[2]) / (es[0] + es[1] + es[2])
    br_a = _dot(ya.astype(BF16), wa_ref[...])
    br_b = _dot(yb_ref[0].astype(BF16), wb_ref[...])
    h = _modulated(x_ref, mod_ref, 0)
    gate_a = _dot(h, wg_ref[:, 0:D_MODEL])
    gate_b = _dot(h, wg_ref[:, D_MODEL:2 * D_MODEL])
    merged = _sigmoid(gate_a) * br_a + _sigmoid(gate_b) * br_b
    mix = _dot(merged.astype(BF16), wo_ref[...])
    gt = mod_ref[0, :, 2 * D_MODEL:3 * D_MODEL]
    z = ALPHA * x_ref[0] + (1.0 + gt) * mix
    out_ref[0] = _ln(z) * g_ref[...] + b_ref[...]


def _merge_call(os, lses, yb, x, mod, wg, wa, wb, wo, layer, ln_g, ln_b, *, tm, name):
    bsz, s, d = x.shape
    dils = tuple(o.shape[1] for o in os)
    tok = lambda w: pl.BlockSpec((1, tm, w), lambda b, i: (b, i, 0))
    res = lambda a: pl.BlockSpec((1, a.shape[1], tm // a.shape[1], a.shape[3]), lambda b, i: (b, 0, i, 0))
    full = lambda a: pl.BlockSpec(a.shape, lambda b, i: (0, 0))
    wsl = lambda a: pl.BlockSpec((None,) + a.shape[1:], lambda b, i: (layer, 0, 0), pipeline_mode=pl.Buffered(1))
    return pl.pallas_call(
        functools.partial(_merge_kernel, dils=dils, tm=tm),
        grid=(bsz, s // tm),
        in_specs=[res(a) for a in os] + [res(a) for a in lses] + [
            tok(B_WIDTH), tok(d), _mod_spec(mod, tm, False),
            wsl(wg), wsl(wa), wsl(wb), wsl(wo), full(ln_g), full(ln_b)],
        out_specs=tok(d),
        out_shape=jax.ShapeDtypeStruct((bsz, s, d), F32),
        scratch_shapes=[pltpu.VMEM((LANE_SLABS, tm, LANES), F32)],
        compiler_params=_cparams(("parallel", "parallel")),
        name=name,
    )(*os, *lses, yb, x, mod, wg, wa, wb, wo, ln_g, ln_b)


def _ffn_kernel(x_ref, mod_ref, wu_ref, wd_ref, g_ref, b_ref, out_ref, act_ref):
    h = _modulated(x_ref, mod_ref, 3)
    for st, sz in FF_SUBTILES:
        a = _dot(h, wu_ref[:, st:st + sz])
        gg = _dot(h, wu_ref[:, FF_DIM + st:FF_DIM + st + sz])
        act_ref[:, st:st + sz] = (_silu(a) * gg).astype(BF16)
    y = _dot(act_ref[...], wd_ref[...])
    gt = mod_ref[0, :, 5 * D_MODEL:6 * D_MODEL]
    z = ALPHA * x_ref[0] + (1.0 + gt) * y
    out_ref[0] = _ln(z) * g_ref[...] + b_ref[...]


def _ffn_call(x, mod, w_up, w_down, layer, ln_g, ln_b, *, tm, name):
    bsz, s, d = x.shape
    assert sum(sz for _, sz in FF_SUBTILES) == FF_DIM
    once = pl.Buffered(1)
    return pl.pallas_call(
        _ffn_kernel,
        grid=(bsz, s // tm),
        in_specs=[
            pl.BlockSpec((1, tm, d), lambda b, i: (b, i, 0)),
            _mod_spec(mod, tm, False),
            pl.BlockSpec((None, d, 2 * FF_DIM), lambda b, i: (layer, 0, 0), pipeline_mode=once),
            pl.BlockSpec((None, FF_DIM, d), lambda b, i: (layer, 0, 0), pipeline_mode=once),
            pl.BlockSpec((1, d), lambda b, i: (0, 0)),
            pl.BlockSpec((1, d), lambda b, i: (0, 0)),
        ],
        out_specs=pl.BlockSpec((1, tm, d), lambda b, i: (b, i, 0)),
        out_shape=jax.ShapeDtypeStruct((bsz, s, d), F32),
        scratch_shapes=[pltpu.VMEM((tm, FF_DIM), BF16)],
        compiler_params=_cparams(("parallel", "arbitrary")),
        name=name,
    )(x, mod, w_up, w_down, ln_g, ln_b)


def kernel(x_prompt, x_sample, c_prompt, c_sample, cache_kv_w128, cache_kv_w512, cache_kv_w2048, state_hgrn,
           w_in, w_branch_a, w_branch_b, w_out, hgrn_norm_w, hgrn_lb_logits, rel_bias,
           ffn_w_up, ffn_w_down, w_ada, b_ada, ln1_g, ln1_b, ln2_g, ln2_b):
    depth = w_in.shape[0]
    bp, sp, d = x_prompt.shape
    bd, sd, _ = x_sample.shape
    assert sd == DEC_ROWS and d == D_MODEL
    max_win = A_WINDOWS[-1]
    assert sp % max_win == 0
    tm_p = 1024
    tm_s = bd * sd
    n_qkv = 3 * A_QKV_WIDTH

    w_in_b = w_in.astype(BF16)
    w_hg = w_in_b[:, :, n_qkv:n_qkv + 4 * B_WIDTH]
    w_gate = w_in_b[:, :, n_qkv + 4 * B_WIDTH:]
    wa_b = w_branch_a.astype(BF16)
    wb_b = w_branch_b.astype(BF16)
    wo_b = w_out.astype(BF16)
    wup_b = ffn_w_up.astype(BF16)
    wdn_b = ffn_w_down.astype(BF16)

    lbs = _lbs_call(hgrn_lb_logits)
    n_c = bp + bd
    n_c_pad = -(-n_c // 8) * 8
    c_all = jnp.concatenate([c_prompt, c_sample, jnp.zeros((n_c_pad - n_c, d), F32)], axis=0)
    mod_all = _ada_call(c_all, w_ada, b_ada)
    bias_p = _bias_p_call(rel_bias)

    caches_t = [c.transpose(0, 1, 2, 4, 5, 3) for c in (cache_kv_w128, cache_kv_w512, cache_kv_w2048)]
    bias_s = [_bias_s_call(rel_bias, g, caches_t[g].shape[-1]) for g in range(A_GROUPS)]
    new_caches = [None] * A_GROUPS
    kv_p = [[] for _ in range(A_GROUPS)]
    s_p, s_s = [], []

    xp = x_prompt
    xs = x_sample.reshape(1, tm_s, d)
    hg_ranges = ((0, 1, F32),)
    sample_ranges = ((0, 9, F32), (9, 4, F32))
    for l in range(depth):
        mod_p = mod_all[l, 0:bp].reshape(bp, 1, N_MOD)
        mod_s = jnp.repeat(mod_all[l, bp:bp + bd], sd, axis=0).reshape(1, tm_s, N_MOD)
        lb = lbs[l:l + 1]
        nw = hgrn_norm_w[l:l + 1].astype(F32)
        g1, b1 = ln1_g[l:l + 1], ln1_b[l:l + 1]
        g2, b2 = ln2_g[l:l + 1], ln2_b[l:l + 1]

        qkv_groups = _qkv_p_call(xp, mod_p, w_in_b, l, tm=tm_p)
        (hg,) = _ln_mm_call(xp, mod_p, w_hg, l, ranges=hg_ranges, tm=tm_p, tn=4 * B_WIDTH, name="hg_proj_prompt")
        tails = _kv_tail_call(xp, mod_p, w_in_b, l)
        os, lses = [], []
        for g in range(A_GROUPS):
            o, lse = _attn_p_call(qkv_groups[g], bias_p, g)
            os.append(o)
            lses.append(lse)
            kv_p[g].append(tails[g])
        yb, st = _hgrn_call(hg, lb, nw, None, layer=l, seq_rows=HGRN_TILE, name="hgrn_prompt")
        s_p.append(st)
        xp = _merge_call(os, lses, yb, xp, mod_p, w_gate, wa_b, wb_b, wo_b, l, g1, b1, tm=tm_p,
                         name="merge_prompt")
        xp = _ffn_call(xp, mod_p, wup_b, wdn_b, l, g2, b2, tm=tm_p, name="ffn_prompt")

        qkv_s, hg_s = _ln_mm_call(xs, mod_s, w_in_b, l, ranges=sample_ranges, tm=tm_s, tn=512,
                                  name="in_proj_sample")
        qkv_s3 = qkv_s.reshape(bd, sd, n_qkv)
        os, lses = [], []
        for g in range(A_GROUPS):
            new_caches[g], o, lse = _attn_s_call(qkv_s3, caches_t[g], new_caches[g], bias_s[g], l, g)
            os.append(o.reshape(1, 1, tm_s, A_GROUP_WIDTH))
            lses.append(lse.reshape(1, 1, tm_s, LANES))
        yb_s, st_s = _hgrn_call(hg_s.reshape(bd, sd, 4 * B_WIDTH), lb, nw, state_hgrn, layer=l, seq_rows=sd,
                                name="hgrn_sample")
        s_s.append(st_s)
        xs = _merge_call(os, lses, yb_s.reshape(1, tm_s, B_WIDTH), xs, mod_s, w_gate, wa_b, wb_b, wo_b, l,
                         g1, b1, tm=tm_s, name="merge_sample")
        xs = _ffn_call(xs, mod_s, wup_b, wdn_b, l, g2, b2, tm=tm_s, name="ffn_sample")

    def rows_last(a):
        return a.transpose(0, 1, 2, 5, 3, 4)

    outs_kv_p = [rows_last(jnp.stack(kv_p[g]).reshape(depth, bp, 2, A_HEADS, A_HEAD_DIM, -1))
                 for g in range(A_GROUPS)]
    outs_kv_s = [rows_last(nc) for nc in new_caches]
    return (xp, xs.reshape(bd, sd, d),
            outs_kv_p[0], outs_kv_p[1], outs_kv_p[2], jnp.stack(s_p),
            outs_kv_s[0], outs_kv_s[1], outs_kv_s[2], jnp.stack(s_s))
```
